```python
import jax, jax.numpy as jnp
from jax import lax
import numpy as np

D_MODEL = 2048
BATCH = 16
SEQ = 256
DEPTH = 4
DEC_BATCH = 8
DEC_SEQ = 1024
PAST_LEN = 256

GRID_W = 64
N_EVEN = (DEPTH + 1) // 2
N_ODD = DEPTH // 2
NA_HEADS = 16
NA_HEAD_DIM = 64
NA_WIDTH = NA_HEADS * NA_HEAD_DIM
NA_KH = 8
NA_KW = 16
CONV_CH = D_MODEL - NA_WIDTH
CONV_K = 3
IN_AB = 3 * NA_WIDTH + 3 * CONV_CH
SGU_WIDTH = D_MODEL
SGU_GROUPS = 16
SGU_GROUP_DIM = SGU_WIDTH // SGU_GROUPS
CHUNK = 128
N_EXPERTS = 32
TOP_K = 4
D_FF = D_MODEL
SWIGLU_LIMIT = 7.0
SWIGLU_ALPHA = 1.702
EPS = 1e-6

kernel_name = "hybrid_natten_shortconv_gmlp_moe_denoise_step"


def rmsnorm(x, g):
    x32 = x.astype(jnp.float32)
    y = x32 * lax.rsqrt(jnp.mean(x32 * x32, axis=-1, keepdims=True) + EPS)
    return y.astype(x.dtype) * g


def modulation(cvec, w_ada, b_ada):
    return jnp.split(jax.nn.silu(cvec) @ w_ada + b_ada, 6, axis=-1)


def softmax_f32(s, dtype):
    return jax.nn.softmax(s.astype(jnp.float32), axis=-1).astype(dtype)


def split_heads(x):
    return x.reshape(x.shape[0], x.shape[1], NA_HEADS, NA_HEAD_DIM)


def short_conv(x, w, b):
    t = x.shape[1]
    xp = jnp.pad(x, ((0, 0), (1, 1), (0, 0)))
    return xp[:, :t] * w[0] + xp[:, 1:t + 1] * w[1] + xp[:, 2:] * w[2] + b


def ab_projection(xm, w_in):
    h = xm @ w_in
    cuts = [NA_WIDTH, 2 * NA_WIDTH, 3 * NA_WIDTH, 3 * NA_WIDTH + CONV_CH, 3 * NA_WIDTH + 2 * CONV_CH]
    return jnp.split(h, cuts, axis=-1)


def context_attention(q, k, v):
    s = jnp.einsum('bqhd,bkhd->bhqk', q, k) * (NA_HEAD_DIM ** -0.5)
    p = softmax_f32(s, v.dtype)
    return jnp.einsum('bhqk,bkhd->bqhd', p, v)


def neighbourhood_attention(q, k, v, kc, vc, rpb):
    b, t = q.shape[0], q.shape[1]
    rows = t // GRID_W
    kh = min(NA_KH, rows)
    rs = jnp.clip(jnp.arange(rows) - kh // 2, 0, rows - kh)
    row_idx = rs[:, None] + jnp.arange(kh)[None, :]
    qg = q.reshape(b, rows, GRID_W, NA_HEADS, NA_HEAD_DIM)
    kg = k.reshape(b, rows, GRID_W, NA_HEADS, NA_HEAD_DIM)[:, row_idx]
    vg = v.reshape(b, rows, GRID_W, NA_HEADS, NA_HEAD_DIM)[:, row_idx]
    scale = NA_HEAD_DIM ** -0.5
    s_loc = jnp.einsum('brqhd,brkwhd->bhrqkw', qg, kg).astype(jnp.float32) * scale
    cols = jnp.arange(GRID_W)
    cs = jnp.clip(cols - NA_KW // 2, 0, GRID_W - NA_KW)
    col_mask = (cols[None, :] >= cs[:, None]) & (cols[None, :] < cs[:, None] + NA_KW)
    dr_idx = row_idx - jnp.arange(rows)[:, None] + (NA_KH - 1)
    dc_idx = jnp.clip(cols[None, :] - cols[:, None], -(NA_KW - 1), NA_KW - 1) + (NA_KW - 1)
    bias = rpb[:, dr_idx[:, None, :, None], dc_idx[None, :, None, :]].astype(jnp.float32)
    s_loc = jnp.where(col_mask[None, None, None, :, None, :], s_loc + bias[None], -jnp.inf)
    s_loc = s_loc.reshape(b, NA_HEADS, rows, GRID_W, kh * GRID_W)
    s_ctx = jnp.einsum('brqhd,bshd->bhrqs', qg, kc).astype(jnp.float32) * scale
    p = softmax_f32(jnp.concatenate([s_loc, s_ctx], axis=-1), v.dtype)
    p_loc, p_ctx = p[..., :kh * GRID_W], p[..., kh * GRID_W:]
    vg = vg.reshape(b, rows, kh * GRID_W, NA_HEADS, NA_HEAD_DIM)
    o = (jnp.einsum('bhrqn,brnhd->brqhd', p_loc, vg)
         + jnp.einsum('bhrqs,bshd->brqhd', p_ctx, vc))
    return o.reshape(b, t, NA_WIDTH)


def ab_mixer_context(xm, w_in, conv_w, conv_b, w_out):
    q, k, v, bg, cg, xs = ab_projection(xm, w_in)
    kh_, vh_ = split_heads(k), split_heads(v)
    att = context_attention(split_heads(q), kh_, vh_).reshape(xm.shape[0], xm.shape[1], NA_WIDTH)
    conv = bg * short_conv(cg * xs, conv_w, conv_b)
    return jnp.concatenate([att, conv], axis=-1) @ w_out, kh_, vh_


def ab_mixer_latent(xm, kc, vc, w_in, conv_w, conv_b, rpb, w_out):
    q, k, v, bg, cg, xs = ab_projection(xm, w_in)
    att = neighbourhood_attention(split_heads(q), split_heads(k), split_heads(v), kc, vc, rpb)
    conv = bg * short_conv(cg * xs, conv_w, conv_b)
    return jnp.concatenate([att, conv], axis=-1) @ w_out


def sgu_mixer(xm, w_in, g_v, w_s, b_s, w_out):
    b, t, _ = xm.shape
    u, v = jnp.split(xm @ w_in, 2, axis=-1)
    v = rmsnorm(v, g_v).reshape(b, t // CHUNK, CHUNK, SGU_GROUPS, SGU_GROUP_DIM)
    v = jnp.einsum('gpq,bnqgc->bnpgc', w_s, v) + b_s.T[:, :, None]
    return (u * v.reshape(b, t, SGU_WIDTH)) @ w_out


def moe(x, w_router, b_router, w1, b1, w2, b2):
    logits = (x @ w_router + b_router).astype(jnp.float32)
    top_v, top_i = lax.top_k(logits, TOP_K)
    top_w = jax.nn.softmax(top_v, axis=-1)
    combine = jnp.sum(jax.nn.one_hot(top_i, N_EXPERTS, dtype=jnp.float32) * top_w[..., None],
                      axis=-2).astype(x.dtype)
    out = jnp.zeros_like(x)
    for e in range(N_EXPERTS):
        h = x @ w1[e] + b1[e]
        gate = jnp.minimum(h[:, :D_FF], SWIGLU_LIMIT)
        up = jnp.clip(h[:, D_FF:], -SWIGLU_LIMIT, SWIGLU_LIMIT)
        y = (gate * jax.nn.sigmoid(SWIGLU_ALPHA * gate) * (up + 1.0)) @ w2[e] + b2[e]
        out = out + combine[:, e:e + 1] * y
    return out


def setup_inputs(seed: int = 0) -> dict:
    key = jax.random.key(seed)
    ks = jax.random.split(key, 32)

    def nrm(k, shape, scale):
        return scale * jax.random.normal(k, shape, jnp.float32)

    return {
        "x_prompt": nrm(ks[0], (BATCH, SEQ, D_MODEL), 1.0),
        "x_sample": nrm(ks[1], (DEC_BATCH, DEC_SEQ, D_MODEL), 1.0),
        "cache_k": nrm(ks[2], (DEC_BATCH, N_EVEN, PAST_LEN, NA_HEADS, NA_HEAD_DIM), 1.0),
        "cache_v": nrm(ks[3], (DEC_BATCH, N_EVEN, PAST_LEN, NA_HEADS, NA_HEAD_DIM), 1.0),
        "c": nrm(ks[4], (DEC_BATCH, D_MODEL), 1.0),
        "c_ctx": nrm(ks[5], (D_MODEL,), 1.0),
        "w_ada": nrm(ks[6], (DEPTH, D_MODEL, 6 * D_MODEL), 0.5 * D_MODEL ** -0.5),
        "b_ada": nrm(ks[7], (DEPTH, 6 * D_MODEL), 0.01),
        "g_mix": 1.0 + nrm(ks[8], (DEPTH, D_MODEL), 0.01),
        "g_ffn": 1.0 + nrm(ks[9], (DEPTH, D_MODEL), 0.01),
        "g_final": 1.0 + nrm(ks[10], (D_MODEL,), 0.01),
        "w_in_ab": nrm(ks[11], (N_EVEN, D_MODEL, IN_AB), D_MODEL ** -0.5),
        "conv_w": nrm(ks[12], (N_EVEN, CONV_K, CONV_CH), CONV_K ** -0.5),
        "conv_b": nrm(ks[13], (N_EVEN, CONV_CH), 0.01),
        "rpb": nrm(ks[14], (N_EVEN, NA_HEADS, 2 * NA_KH - 1, 2 * NA_KW - 1), 0.1),
        "w_out_ab": nrm(ks[15], (N_EVEN, NA_WIDTH + CONV_CH, D_MODEL), (NA_WIDTH + CONV_CH) ** -0.5),
        "w_in_c": nrm(ks[16], (N_ODD, D_MODEL, 2 * SGU_WIDTH), D_MODEL ** -0.5),
        "g_sgu": 1.0 + nrm(ks[17], (N_ODD, SGU_WIDTH), 0.01),
        "w_s": nrm(ks[18], (N_ODD, SGU_GROUPS, CHUNK, CHUNK), CHUNK ** -0.5),
        "b_s": 1.0 + nrm(ks[19], (N_ODD, SGU_GROUPS, CHUNK), 0.01),
        "w_out_c": nrm(ks[20], (N_ODD, SGU_WIDTH, D_MODEL), SGU_WIDTH ** -0.5),
        "w_router": nrm(ks[21], (DEPTH, D_MODEL, N_EXPERTS), D_MODEL ** -0.5),
        "b_router": nrm(ks[22], (DEPTH, N_EXPERTS), 0.01),
        "w1": nrm(ks[23], (DEPTH, N_EXPERTS, D_MODEL, 2 * D_FF), D_MODEL ** -0.5),
        "b1": nrm(ks[24], (DEPTH, N_EXPERTS, 2 * D_FF), 0.01),
        "w2": nrm(ks[25], (DEPTH, N_EXPERTS, D_FF, D_MODEL), D_FF ** -0.5),
        "b2": nrm(ks[26], (DEPTH, N_EXPERTS, D_MODEL), 0.01),
    }


def reference(x_prompt, x_sample, cache_k, cache_v, c, c_ctx, w_ada, b_ada, g_mix, g_ffn, g_final,
              w_in_ab, conv_w, conv_b, rpb, w_out_ab, w_in_c, g_sgu, w_s, b_s, w_out_c,
              w_router, b_router, w1, b1, w2, b2):
    xc, xl = x_prompt, x_sample
    cvec_ctx = c_ctx[None, None, :]
    cvec_lat = c[:, None, :]
    n_ctx = xc.shape[0] * xc.shape[1]
    new_k, new_v = [], []
    for l in range(DEPTH):
        sh1c, sc1c, gt1c, sh2c, sc2c, gt2c = modulation(cvec_ctx, w_ada[l], b_ada[l])
        sh1l, sc1l, gt1l, sh2l, sc2l, gt2l = modulation(cvec_lat, w_ada[l], b_ada[l])
        xmc = rmsnorm(xc, g_mix[l]) * (1.0 + sc1c) + sh1c
        xml = rmsnorm(xl, g_mix[l]) * (1.0 + sc1l) + sh1l
        if l % 2 == 0:
            i = l // 2
            mc, k_ctx, v_ctx = ab_mixer_context(xmc, w_in_ab[i], conv_w[i], conv_b[i], w_out_ab[i])
            ml = ab_mixer_latent(xml, cache_k[:, i], cache_v[:, i], w_in_ab[i], conv_w[i], conv_b[i],
                                 rpb[i], w_out_ab[i])
            new_k.append(k_ctx)
            new_v.append(v_ctx)
        else:
            j = l // 2
            mc = sgu_mixer(xmc, w_in_c[j], g_sgu[j], w_s[j], b_s[j], w_out_c[j])
            ml = sgu_mixer(xml, w_in_c[j], g_sgu[j], w_s[j], b_s[j], w_out_c[j])
        xc = xc + gt1c * mc
        xl = xl + gt1l * ml
        ymc = rmsnorm(xc, g_ffn[l]) * (1.0 + sc2c) + sh2c
        yml = rmsnorm(xl, g_ffn[l]) * (1.0 + sc2l) + sh2l
        flat = jnp.concatenate([ymc.reshape(-1, D_MODEL), yml.reshape(-1, D_MODEL)], axis=0)
        f = moe(flat, w_router[l], b_router[l], w1[l], b1[l], w2[l], b2[l])
        xc = xc + gt2c * f[:n_ctx].reshape(xc.shape)
        xl = xl + gt2l * f[n_ctx:].reshape(xl.shape)
    y_prompt = rmsnorm(xc, g_final)
    y_sample = rmsnorm(xl, g_final)
    new_cache_k = jnp.stack(new_k, axis=1)
    new_cache_v = jnp.stack(new_v, axis=1)
    return (y_prompt, y_sample, new_cache_k, new_cache_v)
```

```python
import functools

import jax
import jax.numpy as jnp
from jax import lax
from jax.experimental import pallas as pl
from jax.experimental.pallas import tpu as pltpu

GRID_W = 64
NA_KH = 8
NA_KW = 16
NA_HEAD_DIM = 64
CHUNK = 128
SGU_GROUP_DIM = 128
TOP_K = 4
SWIGLU_LIMIT = 7.0
SWIGLU_ALPHA = 1.702
EPS = 1e-6

V7X_LANES = 128
V7X_SUBLANES = 8
V7X_VMEM_LIMIT_BYTES = 56 * 1024 * 1024

TOKEN_TILE = 256
PROJ_TILE = 512
COMBINE_TILE = 128
COL_TILE = 1024
MASKED = -1e30

F32 = jnp.float32
BF16 = jnp.bfloat16


def _params(n_axes):
    return pltpu.CompilerParams(
        dimension_semantics=("arbitrary",) * n_axes,
        vmem_limit_bytes=V7X_VMEM_LIMIT_BYTES,
    )


def _dot(a, b):
    return jnp.dot(a, b, preferred_element_type=F32)


def _dot_nt(a, b):
    return lax.dot_general(a, b, (((1,), (1,)), ((), ())), preferred_element_type=F32)


def _split_bf16(a):
    hi = a.astype(BF16)
    lo = (a - hi.astype(F32)).astype(BF16)
    return hi, lo


def _rms_mod(x, g, sc, sh):
    y = x * lax.rsqrt(jnp.mean(x * x, axis=-1, keepdims=True) + EPS)
    return (y * g) * (1.0 + sc) + sh


def _mod_kernel(c_ref, w_ref, b_ref, o_ref):
    c = c_ref[...]
    s_hi, s_lo = _split_bf16(c * jax.nn.sigmoid(c))
    w_hi, w_lo = _split_bf16(w_ref[...])
    o_ref[...] = _dot(s_hi, w_hi) + _dot(s_hi, w_lo) + _dot(s_lo, w_hi) + b_ref[...]


def _modulation(cvec, w_ada, b_ada):
    depth, d, n = w_ada.shape
    rows = cvec.shape[0]
    tn = min(512, n)
    return pl.pallas_call(
        _mod_kernel,
        out_shape=jax.ShapeDtypeStruct((depth, rows, n), F32),
        grid=(depth, n // tn),
        in_specs=[
            pl.BlockSpec((rows, d), lambda l, j: (0, 0)),
            pl.BlockSpec((None, d, tn), lambda l, j: (l, 0, j)),
            pl.BlockSpec((None, 1, tn), lambda l, j: (l, 0, j)),
        ],
        out_specs=pl.BlockSpec((None, rows, tn), lambda l, j: (l, 0, j)),
        compiler_params=_params(2),
        name="modulation",
    )(cvec, w_ada, b_ada.reshape(depth, 1, n))


class _Tokens:
    def __init__(self, n_ctx, seq_ctx, n_lat, seq_lat):
        self.n_ctx, self.seq_ctx, self.n_lat, self.seq_lat = n_ctx, seq_ctx, n_lat, seq_lat
        self.total = n_ctx + n_lat

    def mod_row(self, tile, tm):
        nct = self.n_ctx // tm
        per = self.seq_lat // tm
        return jnp.where(tile < nct, 0, 1 + (tile - nct) // per)

    def row_spec(self, tm, d):
        return pl.BlockSpec((None, 1, d), lambda i, *_: (self.mod_row(i, tm), 0, 0))


def _vec_spec(d):
    return pl.BlockSpec((1, d), lambda i, *_: (0, 0))


def _norm_kernel(x_ref, g_ref, sc_ref, sh_ref, o_ref):
    o_ref[...] = _rms_mod(x_ref[...], g_ref[...], sc_ref[...], sh_ref[...]).astype(o_ref.dtype)


def _first_norm(tok, x, g, sc, sh):
    t, d = x.shape
    tm = TOKEN_TILE
    return pl.pallas_call(
        _norm_kernel,
        out_shape=jax.ShapeDtypeStruct((t, d), BF16),
        grid=(t // tm,),
        in_specs=[pl.BlockSpec((tm, d), lambda i: (i, 0)), _vec_spec(d),
                  tok.row_spec(tm, d), tok.row_spec(tm, d)],
        out_specs=pl.BlockSpec((tm, d), lambda i: (i, 0)),
        compiler_params=_params(1),
        name="first_norm",
    )(x, g, sc, sh)


def _proj_kernel(x_ref, w_ref, o_ref, wb_ref):
    @pl.when(pl.program_id(1) == 0)
    def _():
        wb_ref[...] = w_ref[...].astype(BF16)

    o_ref[...] = _dot(x_ref[...], wb_ref[...])


def _projection(xm, w, idx, cw):
    t, d = xm.shape
    n = w.shape[-1]
    tm = PROJ_TILE
    return pl.pallas_call(
        _proj_kernel,
        out_shape=jax.ShapeDtypeStruct((n // cw, t, cw), F32),
        grid=(n // cw, t // tm),
        in_specs=[pl.BlockSpec((tm, d), lambda j, i: (i, 0)),
                  pl.BlockSpec((None, d, cw), lambda j, i: (idx, 0, j))],
        out_specs=pl.BlockSpec((None, tm, cw), lambda j, i: (j, i, 0)),
        scratch_shapes=[pltpu.VMEM((d, cw), BF16)],
        compiler_params=_params(2),
        name="projection",
    )(xm, w)


def _ctx_attn_kernel(q_ref, k_ref, v_ref, o_ref, *, heads):
    scale = NA_HEAD_DIM ** -0.5
    for h in range(heads):
        sl = slice(h * NA_HEAD_DIM, (h + 1) * NA_HEAD_DIM)
        q = q_ref[:, sl].astype(BF16)
        k = k_ref[:, sl].astype(BF16)
        v = v_ref[:, sl].astype(BF16)
        s = _dot_nt(q, k) * scale
        p = jnp.exp(s - jnp.max(s, axis=-1, keepdims=True))
        p = p / jnp.sum(p, axis=-1, keepdims=True)
        o_ref[:, sl] = _dot(p.astype(BF16), v).astype(o_ref.dtype)


def _context_attention(h6, n_batch, seq, cw):
    return pl.pallas_call(
        functools.partial(_ctx_attn_kernel, heads=cw // NA_HEAD_DIM),
        out_shape=jax.ShapeDtypeStruct((n_batch * seq, cw), BF16),
        grid=(n_batch,),
        in_specs=[pl.BlockSpec((None, seq, cw), lambda b: (0, b, 0)),
                  pl.BlockSpec((None, seq, cw), lambda b: (1, b, 0)),
                  pl.BlockSpec((None, seq, cw), lambda b: (2, b, 0))],
        out_specs=pl.BlockSpec((seq, cw), lambda b: (b, 0)),
        compiler_params=_params(1),
        name="context_attention",
    )(h6, h6, h6)


def _window_start(r, rows, kh):
    return jnp.clip(r - kh // 2, 0, rows - kh)


def _nbr_bias_slabs(rpb, rows):
    kh = min(NA_KH, rows)
    cols = jnp.arange(GRID_W)
    cs = jnp.clip(cols - NA_KW // 2, 0, GRID_W - NA_KW)
    mask = (cols[None, :] >= cs[:, None]) & (cols[None, :] < cs[:, None] + NA_KW)
    dc = jnp.clip(cols[None, :] - cols[:, None], -(NA_KW - 1), NA_KW - 1) + (NA_KW - 1)
    tab = jnp.where(mask[None, None], rpb[:, :, dc].astype(F32), MASKED)
    first = jnp.arange(NA_KH - kh, NA_KH)[:, None] + jnp.arange(kh)[None, :]
    slab = tab[:, first]
    heads = rpb.shape[0]
    return slab.transpose(0, 1, 3, 2, 4).reshape(heads, kh, GRID_W, kh * GRID_W)


def _nbr_attn_kernel(q_ref, k_ref, v_ref, kc_ref, vc_ref, bias_ref, o_ref, *, heads, rows, kh):
    scale = NA_HEAD_DIM ** -0.5
    start = pl.multiple_of(_window_start(pl.program_id(1), rows, kh) * GRID_W, GRID_W)
    win = pl.ds(start, kh * GRID_W)
    for h in range(heads):
        sl = slice(h * NA_HEAD_DIM, (h + 1) * NA_HEAD_DIM)
        q = q_ref[:, sl].astype(BF16)
        s_loc = _dot_nt(q, k_ref[win, sl].astype(BF16)) * scale + bias_ref[h]
        s_ctx = _dot_nt(q, kc_ref[:, sl].astype(BF16)) * scale
        m = jnp.maximum(jnp.max(s_loc, axis=-1, keepdims=True), jnp.max(s_ctx, axis=-1, keepdims=True))
        p_loc = jnp.exp(s_loc - m)
        p_ctx = jnp.exp(s_ctx - m)
        den = jnp.sum(p_loc, axis=-1, keepdims=True) + jnp.sum(p_ctx, axis=-1, keepdims=True)
        o = (_dot((p_loc / den).astype(BF16), v_ref[win, sl].astype(BF16))
             + _dot((p_ctx / den).astype(BF16), vc_ref[:, sl].astype(BF16)))
        o_ref[:, sl] = o.astype(o_ref.dtype)


def _neighbourhood_attention(h6, cache_k, cache_v, idx, bias, tok, n_batch, cw):
    rows = tok.seq_lat // GRID_W
    kh = min(NA_KH, rows)
    past = cache_k.shape[2]
    heads = cw // NA_HEAD_DIM
    ck = cache_k.reshape(cache_k.shape[0], cache_k.shape[1], past, cw)
    cv = cache_v.reshape(cache_v.shape[0], cache_v.shape[1], past, cw)
    assert tok.n_ctx % tok.seq_lat == 0 and rows >= NA_KH
    q0 = tok.n_ctx // GRID_W
    s0 = tok.n_ctx // tok.seq_lat

    def first_rel(r):
        return _window_start(r, rows, kh) - r + (kh - 1)

    return pl.pallas_call(
        functools.partial(_nbr_attn_kernel, heads=heads, rows=rows, kh=kh),
        out_shape=jax.ShapeDtypeStruct((tok.n_lat, cw), BF16),
        grid=(n_batch, rows),
        in_specs=[pl.BlockSpec((None, GRID_W, cw), lambda b, r: (0, q0 + b * rows + r, 0)),
                  pl.BlockSpec((None, tok.seq_lat, cw), lambda b, r: (1, s0 + b, 0)),
                  pl.BlockSpec((None, tok.seq_lat, cw), lambda b, r: (2, s0 + b, 0)),
                  pl.BlockSpec((None, None, past, cw), lambda b, r: (b, idx, 0, 0)),
                  pl.BlockSpec((None, None, past, cw), lambda b, r: (b, idx, 0, 0)),
                  pl.BlockSpec((heads, None, GRID_W, kh * GRID_W), lambda b, r: (0, first_rel(r), 0, 0))],
        out_specs=pl.BlockSpec((GRID_W, cw), lambda b, r: (b * rows + r, 0)),
        compiler_params=_params(2),
        name="neighbourhood_attention",
    )(h6, h6, h6, ck, cv, bias)


def _route_top_k(logits):
    lane = lax.broadcasted_iota(jnp.int32, logits.shape, 1)
    l = logits
    vals, idxs = [], []
    for _ in range(TOP_K):
        m = jnp.max(l, axis=-1, keepdims=True)
        idx = jnp.min(jnp.where(l == m, lane, V7X_LANES), axis=-1, keepdims=True)
        vals.append(m)
        idxs.append(idx)
        l = jnp.where(lane == idx, -jnp.inf, l)
    e = [jnp.exp(v - vals[0]) for v in vals]
    den = e[0]
    for k in range(1, TOP_K):
        den = den + e[k]
    route = jnp.zeros(logits.shape, F32)
    for k in range(TOP_K):
        route = jnp.where(lane == k, e[k] / den, route)
        route = jnp.where(lane == TOP_K + k, idxs[k].astype(F32), route)
    return route


def _residual_norm_route(acc, x_ref, gt_ref, g_ref, sc_ref, sh_ref, wrh_ref, wrl_ref, br_ref,
                         xo_ref, ym_ref, rt_ref):
    xn = x_ref[...] + gt_ref[...] * acc
    xo_ref[...] = xn
    ym = _rms_mod(xn, g_ref[...], sc_ref[...], sh_ref[...])
    ym_ref[...] = ym
    hi, lo = _split_bf16(ym)
    logits = _dot(hi, wrh_ref[...]) + _dot(hi, wrl_ref[...]) + _dot(lo, wrh_ref[...]) + br_ref[...]
    rt_ref[...] = _route_top_k(logits)


def _split_router(wr_ref, wrh_ref, wrl_ref):
    hi, lo = _split_bf16(wr_ref[...])
    wrh_ref[...] = hi
    wrl_ref[...] = lo


def _resident(shape, index):
    return pl.BlockSpec(shape, index, pipeline_mode=pl.Buffered(1))


def _tail_specs(tok, tm, d):
    return [pl.BlockSpec((tm, d), lambda i: (i, 0)), tok.row_spec(tm, d), _vec_spec(d),
            tok.row_spec(tm, d), tok.row_spec(tm, d),
            _resident((d, V7X_LANES), lambda i: (0, 0)), _vec_spec(V7X_LANES)]


def _tail_out(t, tm, d):
    shapes = [jax.ShapeDtypeStruct((t, d), F32), jax.ShapeDtypeStruct((t, d), F32),
              jax.ShapeDtypeStruct((t, V7X_LANES), F32)]
    specs = [pl.BlockSpec((tm, d), lambda i: (i, 0)), pl.BlockSpec((tm, d), lambda i: (i, 0)),
             pl.BlockSpec((tm, V7X_LANES), lambda i: (i, 0))]
    return shapes, specs


def _outproj_ab_kernel(att_ref, bg_ref, cg_ref, xs_ref, cgp_ref, xsp_ref, cgn_ref, xsn_ref,
                       cw_ref, cb_ref, w_ref,
                       x_ref, gt_ref, g_ref, sc_ref, sh_ref, wr_ref, br_ref,
                       xo_ref, ym_ref, rt_ref,
                       wb_ref, wrh_ref, wrl_ref, *, tok, tm):
    i = pl.program_id(0)

    @pl.when(i == 0)
    def _():
        wb_ref[...] = w_ref[...].astype(BF16)
        _split_router(wr_ref, wrh_ref, wrl_ref)

    t0 = i * tm
    is_ctx = t0 < tok.n_ctx
    off = jnp.where(is_ctx, t0 % tok.seq_ctx, (t0 - tok.n_ctx) % tok.seq_lat)
    seq = jnp.where(is_ctx, tok.seq_ctx, tok.seq_lat)
    last = V7X_SUBLANES - 1
    z = cg_ref[...] * xs_ref[...]
    z_before = jnp.where(off > 0, cgp_ref[last:, :] * xsp_ref[last:, :], 0.0)
    z_after = jnp.where(off + tm < seq, cgn_ref[:1, :] * xsn_ref[:1, :], 0.0)
    row = lax.broadcasted_iota(jnp.int32, z.shape, 0)
    z_m1 = jnp.where(row == 0, z_before, pltpu.roll(z, 1, 0))
    z_p1 = jnp.where(row == tm - 1, z_after, pltpu.roll(z, tm - 1, 0))
    conv = z_m1 * cw_ref[0:1, :] + z * cw_ref[1:2, :] + z_p1 * cw_ref[2:3, :] + cb_ref[...]
    conv = (bg_ref[...] * conv).astype(BF16)
    cw = conv.shape[1]
    acc = _dot(att_ref[...], wb_ref[:cw, :]) + _dot(conv, wb_ref[cw:, :])
    _residual_norm_route(acc, x_ref, gt_ref, g_ref, sc_ref, sh_ref, wrh_ref, wrl_ref, br_ref,
                         xo_ref, ym_ref, rt_ref)


def _outproj_ab(tok, att, h6, conv_w, conv_b, w_out, idx, x, gt, g, sc, sh, wr, br):
    t, d = x.shape
    cw = att.shape[1]
    tm = TOKEN_TILE
    assert tok.seq_ctx % tm == 0 and tok.seq_lat % tm == 0
    halo = tm // V7X_SUBLANES
    n_halo = t // V7X_SUBLANES

    def slab(s):
        return pl.BlockSpec((None, tm, cw), lambda i: (s, i, 0))

    def before(s):
        return pl.BlockSpec((None, V7X_SUBLANES, cw), lambda i: (s, jnp.maximum(i * halo - 1, 0), 0))

    def after(s):
        return pl.BlockSpec((None, V7X_SUBLANES, cw),
                            lambda i: (s, jnp.minimum((i + 1) * halo, n_halo - 1), 0))

    out_shapes, out_specs = _tail_out(t, tm, d)
    return pl.pallas_call(
        functools.partial(_outproj_ab_kernel, tok=tok, tm=tm),
        out_shape=out_shapes,
        grid=(t // tm,),
        in_specs=[pl.BlockSpec((tm, cw), lambda i: (i, 0)), slab(3), slab(4), slab(5),
                  before(4), before(5), after(4), after(5),
                  pl.BlockSpec((None, 3, cw), lambda i: (idx, 0, 0)),
                  pl.BlockSpec((None, 1, cw), lambda i: (idx, 0, 0)),
                  _resident((None, 2 * cw, d), lambda i: (idx, 0, 0))] + _tail_specs(tok, tm, d),
        out_specs=out_specs,
        scratch_shapes=[pltpu.VMEM((2 * cw, d), BF16), pltpu.VMEM((d, V7X_LANES), BF16),
                        pltpu.VMEM((d, V7X_LANES), BF16)],
        compiler_params=_params(1),
        name="outproj_attn_conv",
    )(att, h6, h6, h6, h6, h6, h6, h6, conv_w, conv_b.reshape(conv_b.shape[0], 1, cw), w_out,
      x, gt, g, sc, sh, wr, br)


def _outproj_sgu_kernel(u0_ref, u1_ref, v0_ref, v1_ref, gs_ref, ws_ref, bs_ref, w_ref,
                        x_ref, gt_ref, g_ref, sc_ref, sh_ref, wr_ref, br_ref,
                        xo_ref, ym_ref, rt_ref,
                        wb_ref, wsb_ref, gated_ref, wrh_ref, wrl_ref, *, tm):
    @pl.when(pl.program_id(0) == 0)
    def _():
        wb_ref[...] = w_ref[...].astype(BF16)
        wsb_ref[...] = ws_ref[...].astype(BF16)
        _split_router(wr_ref, wrh_ref, wrl_ref)

    half = u0_ref.shape[1]
    d = 2 * half
    v0 = v0_ref[...]
    v1 = v1_ref[...]
    ms = (jnp.sum(v0 * v0, axis=-1, keepdims=True) + jnp.sum(v1 * v1, axis=-1, keepdims=True)) / d
    inv = lax.rsqrt(ms + EPS)
    gd = SGU_GROUP_DIM
    for grp in range(d // gd):
        u_ref, v_ref = (u0_ref, v0_ref) if grp * gd < half else (u1_ref, v1_ref)
        loc = (grp * gd) % half
        vg = (v_ref[:, loc:loc + gd] * inv) * gs_ref[:, grp * gd:(grp + 1) * gd]
        for c in range(tm // CHUNK):
            rows = slice(c * CHUNK, (c + 1) * CHUNK)
            sp = _dot(wsb_ref[grp], vg[rows].astype(BF16)) + bs_ref[grp]
            gated_ref[rows, grp * gd:(grp + 1) * gd] = (u_ref[rows, loc:loc + gd] * sp).astype(BF16)
    acc = _dot(gated_ref[...], wb_ref[...])
    _residual_norm_route(acc, x_ref, gt_ref, g_ref, sc_ref, sh_ref, wrh_ref, wrl_ref, br_ref,
                         xo_ref, ym_ref, rt_ref)


def _outproj_sgu(tok, uv4, g_sgu, w_s, b_s, w_out, idx, x, gt, g, sc, sh, wr, br):
    t, d = x.shape
    half = d // 2
    tm = TOKEN_TILE
    assert tok.seq_ctx % tm == 0 and tok.seq_lat % tm == 0 and tm % CHUNK == 0
    groups = d // SGU_GROUP_DIM
    bs = jnp.broadcast_to(b_s[idx][:, :, None], (groups, CHUNK, SGU_GROUP_DIM))

    def slab(s):
        return pl.BlockSpec((None, tm, half), lambda i: (s, i, 0))

    out_shapes, out_specs = _tail_out(t, tm, d)
    return pl.pallas_call(
        functools.partial(_outproj_sgu_kernel, tm=tm),
        out_shape=out_shapes,
        grid=(t // tm,),
        in_specs=[slab(0), slab(1), slab(2), slab(3),
                  pl.BlockSpec((None, 1, d), lambda i: (idx, 0, 0)),
                  _resident((None, groups, CHUNK, CHUNK), lambda i: (idx, 0, 0, 0)),
                  _resident((groups, CHUNK, SGU_GROUP_DIM), lambda i: (0, 0, 0)),
                  _resident((None, d, d), lambda i: (idx, 0, 0))] + _tail_specs(tok, tm, d),
        out_specs=out_specs,
        scratch_shapes=[pltpu.VMEM((d, d), BF16), pltpu.VMEM((groups, CHUNK, CHUNK), BF16),
                        pltpu.VMEM((tm, d), BF16), pltpu.VMEM((d, V7X_LANES), BF16),
                        pltpu.VMEM((d, V7X_LANES), BF16)],
        compiler_params=_params(1),
        name="outproj_gmlp",
    )(uv4, uv4, uv4, uv4, g_sgu.reshape(g_sgu.shape[0], 1, d), w_s, bs, w_out,
      x, gt, g, sc, sh, wr, br)


def _route_metadata(route, n_experts, tm):
    t = route.shape[0]
    e_flat = route[:, TOP_K:2 * TOP_K].astype(jnp.int32).reshape(-1)
    onehot = (e_flat[:, None] == jnp.arange(n_experts, dtype=jnp.int32)[None, :]).astype(jnp.int32)
    csum = jnp.cumsum(onehot, axis=0)
    rank = jnp.sum((csum - onehot) * onehot, axis=1)
    counts = csum[-1]
    padded = ((counts + tm - 1) // tm) * tm
    ends = jnp.cumsum(padded)
    starts = ends - padded
    pos = (starts[e_flat] + rank).astype(jnp.int32)
    m_tiles = (t * TOP_K) // tm + n_experts
    tile_expert = jnp.searchsorted(ends, jnp.arange(m_tiles, dtype=jnp.int32) * tm, side="right")
    tile_expert = jnp.minimum(tile_expert, n_experts - 1).astype(jnp.int32)
    src = jnp.zeros((m_tiles * tm,), jnp.int32).at[pos].set(
        jnp.arange(t * TOP_K, dtype=jnp.int32) // TOP_K)
    return pos, tile_expert, src


def _gather_kernel(src_ref, ym_hbm, o_ref, buf, sem, *, tm, n_tiles):
    i = pl.program_id(0)

    def issue(tile, slot):
        base = tile * tm

        def body(r, carry):
            tok = src_ref[base + r]
            pltpu.make_async_copy(ym_hbm.at[pl.ds(tok, 1)], buf.at[slot, pl.ds(r, 1)],
                                  sem.at[slot]).start()
            return carry

        lax.fori_loop(0, tm, body, 0, unroll=8)

    @pl.when(i == 0)
    def _():
        issue(0, 0)

    @pl.when(i + 1 < n_tiles)
    def _():
        issue(i + 1, (i + 1) % 2)

    slot = i % 2
    pltpu.make_async_copy(ym_hbm.at[pl.ds(0, tm)], buf.at[slot], sem.at[slot]).wait()
    o_ref[...] = buf[slot].astype(o_ref.dtype)


def _dispatch_gather(src, ym, tm):
    t, d = ym.shape
    n_tiles = src.shape[0] // tm
    return pl.pallas_call(
        functools.partial(_gather_kernel, tm=tm, n_tiles=n_tiles),
        out_shape=jax.ShapeDtypeStruct((n_tiles * tm, d), BF16),
        grid_spec=pltpu.PrefetchScalarGridSpec(
            num_scalar_prefetch=1,
            grid=(n_tiles,),
            in_specs=[pl.BlockSpec(memory_space=pl.ANY)],
            out_specs=pl.BlockSpec((tm, d), lambda i, src: (i, 0)),
            scratch_shapes=[pltpu.VMEM((2, tm, d), F32), pltpu.SemaphoreType.DMA((2,))],
        ),
        compiler_params=_params(1),
        name="moe_dispatch_gather",
    )(src, ym)


def _expert_changed(te_ref, m):
    return jnp.logical_or(m == 0, te_ref[m] != te_ref[jnp.maximum(m - 1, 0)])


def _gmm1_kernel(te_ref, x_ref, wg_ref, wu_ref, bg_ref, bu_ref, o_ref, wgb_ref, wub_ref):
    @pl.when(_expert_changed(te_ref, pl.program_id(1)))
    def _():
        wgb_ref[...] = wg_ref[...].astype(BF16)
        wub_ref[...] = wu_ref[...].astype(BF16)

    x = x_ref[...]
    gate = jnp.minimum(_dot(x, wgb_ref[...]) + bg_ref[...], SWIGLU_LIMIT)
    up = jnp.clip(_dot(x, wub_ref[...]) + bu_ref[...], -SWIGLU_LIMIT, SWIGLU_LIMIT)
    o_ref[...] = (gate * jax.nn.sigmoid(SWIGLU_ALPHA * gate) * (up + 1.0)).astype(o_ref.dtype)


def _gmm1(te, xs, w1, b1, layer, tm):
    m_rows, d = xs.shape
    n_experts, f = w1.shape[1], w1.shape[3] // 2
    fn = min(COL_TILE, f)
    nf = f // fn
    b1r = b1.reshape(b1.shape[0], n_experts, 1, 2 * f)
    return pl.pallas_call(
        _gmm1_kernel,
        out_shape=jax.ShapeDtypeStruct((m_rows, f), BF16),
        grid_spec=pltpu.PrefetchScalarGridSpec(
            num_scalar_prefetch=1,
            grid=(nf, m_rows // tm),
            in_specs=[pl.BlockSpec((tm, d), lambda n, m, te: (m, 0)),
                      pl.BlockSpec((None, None, d, fn), lambda n, m, te: (layer, te[m], 0, n)),
                      pl.BlockSpec((None, None, d, fn), lambda n, m, te: (layer, te[m], 0, nf + n)),
                      pl.BlockSpec((None, None, 1, fn), lambda n, m, te: (layer, te[m], 0, n)),
                      pl.BlockSpec((None, None, 1, fn), lambda n, m, te: (layer, te[m], 0, nf + n))],
            out_specs=pl.BlockSpec((tm, fn), lambda n, m, te: (m, n)),
            scratch_shapes=[pltpu.VMEM((d, fn), BF16), pltpu.VMEM((d, fn), BF16)],
        ),
        compiler_params=_params(2),
        name="moe_expert_up",
    )(te, xs, w1, w1, b1r, b1r)


def _gmm2_kernel(te_ref, a_ref, w_ref, b_ref, o_ref, wb_ref):
    @pl.when(_expert_changed(te_ref, pl.program_id(1)))
    def _():
        wb_ref[...] = w_ref[...].astype(BF16)

    o_ref[...] = _dot(a_ref[...], wb_ref[...]) + b_ref[...]


def _gmm2(te, a, w2, b2, layer, tm):
    m_rows, f = a.shape
    n_experts, d = w2.shape[1], w2.shape[3]
    dn = min(COL_TILE, d)
    b2r = b2.reshape(b2.shape[0], n_experts, 1, d)
    return pl.pallas_call(
        _gmm2_kernel,
        out_shape=jax.ShapeDtypeStruct((m_rows, d), F32),
        grid_spec=pltpu.PrefetchScalarGridSpec(
            num_scalar_prefetch=1,
            grid=(d // dn, m_rows // tm),
            in_specs=[pl.BlockSpec((tm, f), lambda n, m, te: (m, 0)),
                      pl.BlockSpec((None, None, f, dn), lambda n, m, te: (layer, te[m], 0, n)),
                      pl.BlockSpec((None, None, 1, dn), lambda n, m, te: (layer, te[m], 0, n))],
            out_specs=pl.BlockSpec((tm, dn), lambda n, m, te: (m, n)),
            scratch_shapes=[pltpu.VMEM((f, dn), BF16)],
        ),
        compiler_params=_params(2),
        name="moe_expert_down",
    )(te, a, w2, b2r)


def _combine_kernel(pos_ref, y_hbm, x_ref, rt_ref, gt_ref, g_ref, sc_ref, sh_ref, *rest,
                    tmc, n_tiles, final):
    if final:
        out_ref, buf, sem = rest
    else:
        xo_ref, out_ref, buf, sem = rest
    i = pl.program_id(0)

    def issue(tile, slot):
        base = tile * (tmc * TOP_K)

        def body(t, carry):
            for k in range(TOP_K):
                p = pos_ref[base + t * TOP_K + k]
                pltpu.make_async_copy(y_hbm.at[pl.ds(p, 1)], buf.at[slot, pl.ds(k * tmc + t, 1)],
                                      sem.at[slot]).start()
            return carry

        lax.fori_loop(0, tmc, body, 0, unroll=2)

    @pl.when(i == 0)
    def _():
        issue(0, 0)

    @pl.when(i + 1 < n_tiles)
    def _():
        issue(i + 1, (i + 1) % 2)

    slot = i % 2
    pltpu.make_async_copy(y_hbm.at[pl.ds(0, TOP_K * tmc)], buf.at[slot], sem.at[slot]).wait()
    f = rt_ref[:, 0:1] * buf[slot, pl.ds(0, tmc), :]
    for k in range(1, TOP_K):
        f = f + rt_ref[:, k:k + 1] * buf[slot, pl.ds(k * tmc, tmc), :]
    xn = x_ref[...] + gt_ref[...] * f
    if final:
        y = xn * lax.rsqrt(jnp.mean(xn * xn, axis=-1, keepdims=True) + EPS)
        out_ref[...] = y * g_ref[...]
    else:
        xo_ref[...] = xn
        out_ref[...] = _rms_mod(xn, g_ref[...], sc_ref[...], sh_ref[...]).astype(out_ref.dtype)


def _combine(tok, pos, y, x, route, gt, g, sc, sh, final):
    t, d = x.shape
    tmc = COMBINE_TILE
    n_tiles = t // tmc
    tile = pl.BlockSpec((tmc, d), lambda i, pos: (i, 0))
    if final:
        out_shape = jax.ShapeDtypeStruct((t, d), F32)
        out_specs = tile
    else:
        out_shape = [jax.ShapeDtypeStruct((t, d), F32), jax.ShapeDtypeStruct((t, d), BF16)]
        out_specs = [tile, tile]
    return pl.pallas_call(
        functools.partial(_combine_kernel, tmc=tmc, n_tiles=n_tiles, final=final),
        out_shape=out_shape,
        grid_spec=pltpu.PrefetchScalarGridSpec(
            num_scalar_prefetch=1,
            grid=(n_tiles,),
            in_specs=[pl.BlockSpec(memory_space=pl.ANY), tile,
                      pl.BlockSpec((tmc, V7X_LANES), lambda i, pos: (i, 0)),
                      tok.row_spec(tmc, d), _vec_spec(d), tok.row_spec(tmc, d), tok.row_spec(tmc, d)],
            out_specs=out_specs,
            scratch_shapes=[pltpu.VMEM((2, TOP_K * tmc, d), F32), pltpu.SemaphoreType.DMA((2,))],
        ),
        compiler_params=_params(1),
        name="moe_combine",
    )(pos, y, x, route, gt, g, sc, sh)


def kernel(x_prompt, x_sample, cache_k, cache_v, c, c_ctx, w_ada, b_ada, g_mix, g_ffn, g_final,
           w_in_ab, conv_w, conv_b, rpb, w_out_ab, w_in_c, g_sgu, w_s, b_s, w_out_c,
           w_router, b_router, w1, b1, w2, b2):
    n_b, seq, d = x_prompt.shape
    n_db, dseq, _ = x_sample.shape
    depth = w_ada.shape[0]
    n_experts = w_router.shape[-1]
    heads, head_dim = cache_k.shape[3], cache_k.shape[4]
    na_width = heads * head_dim
    assert head_dim == NA_HEAD_DIM and w_in_ab.shape[-1] == 6 * na_width and d == 2 * na_width
    tok = _Tokens(n_b * seq, seq, n_db * dseq, dseq)

    x = jnp.concatenate([x_prompt.reshape(tok.n_ctx, d), x_sample.reshape(tok.n_lat, d)], axis=0)

    n_rows = 1 + n_db
    pad_rows = -n_rows % (2 * V7X_SUBLANES)
    cvec = jnp.concatenate([c_ctx[None, :], c, jnp.zeros((pad_rows, d), F32)], axis=0)
    mod = _modulation(cvec, w_ada, b_ada)[:, :n_rows]
    mod = mod.reshape(depth, n_rows, 6, 1, d).transpose(0, 2, 1, 3, 4)

    wr_pad = jnp.pad(w_router, ((0, 0), (0, 0), (0, V7X_LANES - n_experts)))
    br_pad = jnp.pad(b_router, ((0, 0), (0, V7X_LANES - n_experts)), constant_values=MASKED)
    rows = dseq // GRID_W

    xm = _first_norm(tok, x, g_mix[0][None], mod[0, 1], mod[0, 0])
    new_k, new_v = [], []
    out = None
    for l in range(depth):
        sh1, sc1, gt1, sh2, sc2, gt2 = (mod[l, s] for s in range(6))
        tail = (x, gt1, g_ffn[l][None], sc2, sh2, wr_pad[l], br_pad[l][None])
        if l % 2 == 0:
            i = l // 2
            h6 = _projection(xm, w_in_ab, i, na_width)
            att_c = _context_attention(h6, n_b, seq, na_width)
            att_l = _neighbourhood_attention(h6, cache_k, cache_v, i, _nbr_bias_slabs(rpb[i], rows),
                                             tok, n_db, na_width)
            att = jnp.concatenate([att_c, att_l], axis=0)
            x, ym, route = _outproj_ab(tok, att, h6, conv_w, conv_b, w_out_ab, i, *tail)
            new_k.append(h6[1, :tok.n_ctx].reshape(n_b, seq, heads, head_dim))
            new_v.append(h6[2, :tok.n_ctx].reshape(n_b, seq, heads, head_dim))
        else:
            j = l // 2
            uv4 = _projection(xm, w_in_c, j, d // 2)
            x, ym, route = _outproj_sgu(tok, uv4, g_sgu, w_s, b_s, w_out_c, j, *tail)
        pos, tile_expert, src = _route_metadata(route, n_experts, TOKEN_TILE)
        xs = _dispatch_gather(src, ym, TOKEN_TILE)
        a = _gmm1(tile_expert, xs, w1, b1, l, TOKEN_TILE)
        y = _gmm2(tile_expert, a, w2, b2, l, TOKEN_TILE)
        if l + 1 < depth:
            x, xm = _combine(tok, pos, y, x, route, gt2, g_mix[l + 1][None],
                             mod[l + 1, 1], mod[l + 1, 0], final=False)
        else:
            out = _combine(tok, pos, y, x, route, gt2, g_final[None], gt2, gt2, final=True)

    y_prompt = out[:tok.n_ctx].reshape(n_b, seq, d)
    y_sample = out[tok.n_ctx:].reshape(n_db, dseq, d)
    return (y_prompt, y_sample, jnp.stack(new_k, axis=1), jnp.stack(new_v, axis=1))
```

```python
import functools

import jax
import jax.numpy as jnp
from jax import lax
from jax.experimental import pallas as pl
from jax.experimental.pallas import tpu as pltpu

GRID_W = 64
NA_KH = 8
NA_KW = 16
NA_HEAD_DIM = 64
CHUNK = 128
SGU_GROUP_DIM = 128
TOP_K = 4
SWIGLU_LIMIT = 7.0
SWIGLU_ALPHA = 1.702
EPS = 1e-6

V7X_LANES = 128
V7X_SUBLANES = 8
V7X_VMEM_LIMIT_BYTES = 56 * 1024 * 1024

TOKEN_TILE = 256
PROJ_TILE = 512
COMBINE_TILE = 128
COL_TILE = 1024
MASKED = -1e30

F32 = jnp.float32
BF16 = jnp.bfloat16


def _params(n_axes):
    return pltpu.CompilerParams(
        dimension_semantics=("arbitrary",) * n_axes,
        vmem_limit_bytes=V7X_VMEM_LIMIT_BYTES,
    )


def _dot(a, b):
    return jnp.dot(a, b, preferred_element_type=F32)


def _dot_nt(a, b):
    return lax.dot_general(a, b, (((1,), (1,)), ((), ())), preferred_element_type=F32)


def _split_bf16(a):
    hi = a.astype(BF16)
    lo = (a - hi.astype(F32)).astype(BF16)
    return hi, lo


def _rms_mod(x, g, sc, sh):
    y = x * lax.rsqrt(jnp.mean(x * x, axis=-1, keepdims=True) + EPS)
    return (y * g) * (1.0 + sc) + sh


def _mod_kernel(c_ref, w_ref, b_ref, o_ref):
    c = c_ref[...]
    s_hi, s_lo = _split_bf16(c * jax.nn.sigmoid(c))
    w_hi, w_lo = _split_bf16(w_ref[...])
    o_ref[...] = _dot(s_hi, w_hi) + _dot(s_hi, w_lo) + _dot(s_lo, w_hi) + b_ref[...]


def _modulation(cvec, w_ada, b_ada):
    depth, d, n = w_ada.shape
    rows = cvec.shape[0]
    tn = min(512, n)
    return pl.pallas_call(
        _mod_kernel,
        out_shape=jax.ShapeDtypeStruct((depth, rows, n), F32),
        grid=(depth, n // tn),
        in_specs=[
            pl.BlockSpec((rows, d), lambda l, j: (0, 0)),
            pl.BlockSpec((None, d, tn), lambda l, j: (l, 0, j)),
            pl.BlockSpec((None, 1, tn), lambda l, j: (l, 0, j)),
        ],
        out_specs=pl.BlockSpec((None, rows, tn), lambda l, j: (l, 0, j)),
        compiler_params=_params(2),
        name="modulation",
    )(cvec, w_ada, b_ada.reshape(depth, 1, n))


class _Tokens:
    def __init__(self, n_ctx, seq_ctx, n_lat, seq_lat):
        self.n_ctx, self.seq_ctx, self.n_lat, self.seq_lat = n_ctx, seq_ctx, n_lat, seq_lat
        self.total = n_ctx + n_lat

    def mod_row(self, tile, tm):
        nct = self.n_ctx // tm
        per = self.seq_lat // tm
        return jnp.where(tile < nct, 0, 1 + (tile - nct) // per)

    def row_spec(self, tm, d):
        return pl.BlockSpec((None, 1, d), lambda i, *_: (self.mod_row(i, tm), 0, 0))


def _vec_spec(d):
    return pl.BlockSpec((1, d), lambda i, *_: (0, 0))


def _norm_kernel(x_ref, g_ref, sc_ref, sh_ref, o_ref):
    o_ref[...] = _rms_mod(x_ref[...], g_ref[...], sc_ref[...], sh_ref[...]).astype(o_ref.dtype)


def _first_norm(tok, x, g, sc, sh):
    t, d = x.shape
    tm = TOKEN_TILE
    return pl.pallas_call(
        _norm_kernel,
        out_shape=jax.ShapeDtypeStruct((t, d), BF16),
        grid=(t // tm,),
        in_specs=[pl.BlockSpec((tm, d), lambda i: (i, 0)), _vec_spec(d),
                  tok.row_spec(tm, d), tok.row_spec(tm, d)],
        out_specs=pl.BlockSpec((tm, d), lambda i: (i, 0)),
        compiler_params=_params(1),
        name="first_norm",
    )(x, g, sc, sh)


def _proj_kernel(x_ref, w_ref, o_ref, wb_ref):
    @pl.when(pl.program_id(1) == 0)
    def _():
        wb_ref[...] = w_ref[...].astype(BF16)

    o_ref[...] = _dot(x_ref[...], wb_ref[...])


def _projection(xm, w, idx, cw):
    t, d = xm.shape
    n = w.shape[-1]
    tm = PROJ_TILE
    return pl.pallas_call(
        _proj_kernel,
        out_shape=jax.ShapeDtypeStruct((n // cw, t, cw), F32),
        grid=(n // cw, t // tm),
        in_specs=[pl.BlockSpec((tm, d), lambda j, i: (i, 0)),
                  pl.BlockSpec((None, d, cw), lambda j, i: (idx, 0, j))],
        out_specs=pl.BlockSpec((None, tm, cw), lambda j, i: (j, i, 0)),
        scratch_shapes=[pltpu.VMEM((d, cw), BF16)],
        compiler_params=_params(2),
        name="projection",
    )(xm, w)


def _ctx_attn_kernel(q_ref, k_ref, v_ref, o_ref, *, heads):
    scale = NA_HEAD_DIM ** -0.5
    for h in range(heads):
        sl = slice(h * NA_HEAD_DIM, (h + 1) * NA_HEAD_DIM)
        q = q_ref[:, sl].astype(BF16)
        k = k_ref[:, sl].astype(BF16)
        v = v_ref[:, sl].astype(BF16)
        s = _dot_nt(q, k) * scale
        p = jnp.exp(s - jnp.max(s, axis=-1, keepdims=True))
        p = p / jnp.sum(p, axis=-1, keepdims=True)
        o_ref[:, sl] = _dot(p.astype(BF16), v).astype(o_ref.dtype)


def _context_attention(h6, n_batch, seq, cw):
    return pl.pallas_call(
        functools.partial(_ctx_attn_kernel, heads=cw // NA_HEAD_DIM),
        out_shape=jax.ShapeDtypeStruct((n_batch * seq, cw), BF16),
        grid=(n_batch,),
        in_specs=[pl.BlockSpec((None, seq, cw), lambda b: (0, b, 0)),
                  pl.BlockSpec((None, seq, cw), lambda b: (1, b, 0)),
                  pl.BlockSpec((None, seq, cw), lambda b: (2, b, 0))],
        out_specs=pl.BlockSpec((seq, cw), lambda b: (b, 0)),
        compiler_params=_params(1),
        name="context_attention",
    )(h6, h6, h6)


NBR_QUERY_ROWS = 8
NBR_KEY_ROWS = NBR_QUERY_ROWS + NA_KH // 2
HEADS_PER_STEP = V7X_LANES // NA_HEAD_DIM


def _nbr_key_start(blk, rows):
    if isinstance(blk, int):
        return min(max(blk * NBR_QUERY_ROWS - NA_KH // 2, 0), rows - NBR_KEY_ROWS)
    return jnp.clip(blk * NBR_QUERY_ROWS - NA_KH // 2, 0, rows - NBR_KEY_ROWS)


def _nbr_bias_blocks(rpb, rows):
    heads = rpb.shape[0]
    n_blk = rows // NBR_QUERY_ROWS
    ext = jnp.pad(rpb.astype(F32), ((0, 0), (0, 0), (GRID_W - NA_KW, GRID_W - NA_KW)), mode="edge")
    tab = jnp.stack([ext[:, :, GRID_W - 1 - qc:2 * GRID_W - 1 - qc] for qc in range(GRID_W)], axis=2)
    cols = jnp.arange(GRID_W)
    cs = jnp.clip(cols - NA_KW // 2, 0, GRID_W - NA_KW)
    col_ok = (cols[None, :] >= cs[:, None]) & (cols[None, :] < cs[:, None] + NA_KW)
    tab = jnp.where(col_ok[None, None], tab, MASKED)
    masked = jnp.full((heads, GRID_W, GRID_W), MASKED, F32)
    blocks = []
    for blk in range(n_blk):
        k0 = _nbr_key_start(blk, rows)
        q_rows = []
        for qr in range(blk * NBR_QUERY_ROWS, (blk + 1) * NBR_QUERY_ROWS):
            rs = min(max(qr - NA_KH // 2, 0), rows - NA_KH)
            q_rows.append(jnp.concatenate(
                [tab[:, kr - qr + NA_KH - 1] if rs <= kr < rs + NA_KH else masked
                 for kr in range(k0, k0 + NBR_KEY_ROWS)], axis=-1))
        blocks.append(jnp.concatenate(q_rows, axis=1))
    return jnp.stack(blocks, axis=1)


def _nbr_attn_kernel(q_ref, k_ref, v_ref, kc_ref, vc_ref, bias_ref, o_ref, *, rows):
    scale = NA_HEAD_DIM ** -0.5
    start = pl.multiple_of(_nbr_key_start(pl.program_id(0), rows) * GRID_W, GRID_W)
    win = pl.ds(start, NBR_KEY_ROWS * GRID_W)
    for h in range(HEADS_PER_STEP):
        sl = slice(h * NA_HEAD_DIM, (h + 1) * NA_HEAD_DIM)
        q = (q_ref[:, sl] * scale).astype(BF16)
        s_loc = _dot_nt(q, k_ref[win, sl].astype(BF16)) + bias_ref[h]
        s_ctx = _dot_nt(q, kc_ref[:, sl].astype(BF16))
        m = jnp.maximum(jnp.max(s_loc, axis=-1, keepdims=True), jnp.max(s_ctx, axis=-1, keepdims=True))
        p_loc = jnp.exp(s_loc - m)
        p_ctx = jnp.exp(s_ctx - m)
        inv = 1.0 / (jnp.sum(p_loc, axis=-1, keepdims=True) + jnp.sum(p_ctx, axis=-1, keepdims=True))
        o = (_dot((p_loc * inv).astype(BF16), v_ref[win, sl].astype(BF16))
             + _dot((p_ctx * inv).astype(BF16), vc_ref[:, sl].astype(BF16)))
        o_ref[:, sl] = o.astype(o_ref.dtype)


def _neighbourhood_attention(h6, cache_k, cache_v, idx, bias, tok, n_batch, cw):
    rows = tok.seq_lat // GRID_W
    past = cache_k.shape[2]
    heads = cw // NA_HEAD_DIM
    ck = cache_k.reshape(cache_k.shape[0], cache_k.shape[1], past, cw)
    cv = cache_v.reshape(cache_v.shape[0], cache_v.shape[1], past, cw)
    assert tok.n_ctx % tok.seq_lat == 0 and heads % HEADS_PER_STEP == 0
    assert rows % NBR_QUERY_ROWS == 0 and rows >= NBR_KEY_ROWS
    n_blk = rows // NBR_QUERY_ROWS
    for blk in range(n_blk):
        k0 = _nbr_key_start(blk, rows)
        for r in range(blk * NBR_QUERY_ROWS, (blk + 1) * NBR_QUERY_ROWS):
            rs = min(max(r - NA_KH // 2, 0), rows - NA_KH)
            assert k0 <= rs and rs + NA_KH <= k0 + NBR_KEY_ROWS
    tq = NBR_QUERY_ROWS * GRID_W
    q0 = tok.n_ctx // tq
    s0 = tok.n_ctx // tok.seq_lat
    lanes = V7X_LANES

    return pl.pallas_call(
        functools.partial(_nbr_attn_kernel, rows=rows),
        out_shape=jax.ShapeDtypeStruct((tok.n_lat, cw), BF16),
        grid=(n_blk, heads // HEADS_PER_STEP, n_batch),
        in_specs=[pl.BlockSpec((None, tq, lanes), lambda k, h, b: (0, q0 + b * n_blk + k, h)),
                  pl.BlockSpec((None, tok.seq_lat, lanes), lambda k, h, b: (1, s0 + b, h)),
                  pl.BlockSpec((None, tok.seq_lat, lanes), lambda k, h, b: (2, s0 + b, h)),
                  pl.BlockSpec((None, None, past, lanes), lambda k, h, b: (b, idx, 0, h)),
                  pl.BlockSpec((None, None, past, lanes), lambda k, h, b: (b, idx, 0, h)),
                  pl.BlockSpec((HEADS_PER_STEP, None, tq, NBR_KEY_ROWS * GRID_W),
                               lambda k, h, b: (h, k, 0, 0))],
        out_specs=pl.BlockSpec((tq, lanes), lambda k, h, b: (b * n_blk + k, h)),
        compiler_params=_params(3),
        name="neighbourhood_attention",
    )(h6, h6, h6, ck, cv, bias)


def _route_top_k(logits, seen_ref):
    tm = logits.shape[0]
    lane = lax.broadcasted_iota(jnp.int32, logits.shape, 1)
    l = logits
    vals, picks = [], []
    for _ in range(TOP_K):
        m = jnp.max(l, axis=-1, keepdims=True)
        idx = jnp.min(jnp.where(l == m, lane, V7X_LANES), axis=-1, keepdims=True)
        vals.append(m)
        picks.append(lane == idx)
        l = jnp.where(picks[-1], -jnp.inf, l)
    e = [jnp.exp(v - vals[0]) for v in vals]
    den = e[0]
    picked = picks[0]
    for k in range(1, TOP_K):
        den = den + e[k]
        picked = jnp.logical_or(picked, picks[k])
    onehot = picked.astype(F32)
    earlier = (lax.broadcasted_iota(jnp.int32, (tm, tm), 1)
               < lax.broadcasted_iota(jnp.int32, (tm, tm), 0)).astype(BF16)
    before = seen_ref[0:1, :] + _dot(earlier, onehot.astype(BF16))
    lane_f = lane.astype(F32)
    route = jnp.zeros(logits.shape, F32)
    for k in range(TOP_K):
        idx_f = jnp.sum(jnp.where(picks[k], lane_f, 0.0), axis=-1, keepdims=True)
        rank = jnp.sum(jnp.where(picks[k], before, 0.0), axis=-1, keepdims=True)
        route = jnp.where(lane == k, e[k] / den, route)
        route = jnp.where(lane == TOP_K + k, idx_f, route)
        route = jnp.where(lane == 2 * TOP_K + k, rank, route)
    seen = seen_ref[0:1, :] + jnp.sum(onehot, axis=0, keepdims=True)
    seen_ref[...] = jnp.broadcast_to(seen, seen_ref.shape)
    return route


def _residual_norm_route(acc, x_ref, gt_ref, g_ref, sc_ref, sh_ref, wrh_ref, wrl_ref, br_ref,
                         xo_ref, ym_ref, rt_ref, cnt_ref, seen_ref):
    xn = x_ref[...] + gt_ref[...] * acc
    xo_ref[...] = xn
    ym = _rms_mod(xn, g_ref[...], sc_ref[...], sh_ref[...])
    ym_ref[...] = ym
    hi, lo = _split_bf16(ym)
    logits = _dot(hi, wrh_ref[...]) + _dot(hi, wrl_ref[...]) + _dot(lo, wrh_ref[...]) + br_ref[...]
    rt_ref[...] = _route_top_k(logits, seen_ref)
    cnt_ref[...] = seen_ref[...]


def _init_router(wr_ref, wrh_ref, wrl_ref, seen_ref):
    hi, lo = _split_bf16(wr_ref[...])
    wrh_ref[...] = hi
    wrl_ref[...] = lo
    seen_ref[...] = jnp.zeros(seen_ref.shape, F32)


def _resident(shape, index):
    return pl.BlockSpec(shape, index, pipeline_mode=pl.Buffered(1))


def _tail_specs(tok, tm, d):
    return [pl.BlockSpec((tm, d), lambda i: (i, 0)), tok.row_spec(tm, d), _vec_spec(d),
            tok.row_spec(tm, d), tok.row_spec(tm, d),
            _resident((d, V7X_LANES), lambda i: (0, 0)), _vec_spec(V7X_LANES)]


def _tail_out(t, tm, d):
    shapes = [jax.ShapeDtypeStruct((t, d), F32), jax.ShapeDtypeStruct((t, d), F32),
              jax.ShapeDtypeStruct((t, V7X_LANES), F32),
              jax.ShapeDtypeStruct((V7X_SUBLANES, V7X_LANES), F32)]
    specs = [pl.BlockSpec((tm, d), lambda i: (i, 0)), pl.BlockSpec((tm, d), lambda i: (i, 0)),
             pl.BlockSpec((tm, V7X_LANES), lambda i: (i, 0)),
             pl.BlockSpec((V7X_SUBLANES, V7X_LANES), lambda i: (0, 0))]
    return shapes, specs


def _tail_scratch(d):
    return [pltpu.VMEM((d, V7X_LANES), BF16), pltpu.VMEM((d, V7X_LANES), BF16),
            pltpu.VMEM((V7X_SUBLANES, V7X_LANES), F32)]


def _outproj_ab_kernel(att_ref, bg_ref, cg_ref, xs_ref, cgp_ref, xsp_ref, cgn_ref, xsn_ref,
                       cw_ref, cb_ref, w_ref,
                       x_ref, gt_ref, g_ref, sc_ref, sh_ref, wr_ref, br_ref,
                       xo_ref, ym_ref, rt_ref, cnt_ref,
                       wb_ref, wrh_ref, wrl_ref, seen_ref, *, tok, tm):
    i = pl.program_id(0)

    @pl.when(i == 0)
    def _():
        wb_ref[...] = w_ref[...].astype(BF16)
        _init_router(wr_ref, wrh_ref, wrl_ref, seen_ref)

    t0 = i * tm
    is_ctx = t0 < tok.n_ctx
    off = jnp.where(is_ctx, t0 % tok.seq_ctx, (t0 - tok.n_ctx) % tok.seq_lat)
    seq = jnp.where(is_ctx, tok.seq_ctx, tok.seq_lat)
    last = V7X_SUBLANES - 1
    z = cg_ref[...] * xs_ref[...]
    z_before = jnp.where(off > 0, cgp_ref[last:, :] * xsp_ref[last:, :], 0.0)
    z_after = jnp.where(off + tm < seq, cgn_ref[:1, :] * xsn_ref[:1, :], 0.0)
    row = lax.broadcasted_iota(jnp.int32, z.shape, 0)
    z_m1 = jnp.where(row == 0, z_before, pltpu.roll(z, 1, 0))
    z_p1 = jnp.where(row == tm - 1, z_after, pltpu.roll(z, tm - 1, 0))
    conv = z_m1 * cw_ref[0:1, :] + z * cw_ref[1:2, :] + z_p1 * cw_ref[2:3, :] + cb_ref[...]
    conv = (bg_ref[...] * conv).astype(BF16)
    cw = conv.shape[1]
    acc = _dot(att_ref[...], wb_ref[:cw, :]) + _dot(conv, wb_ref[cw:, :])
    _residual_norm_route(acc, x_ref, gt_ref, g_ref, sc_ref, sh_ref, wrh_ref, wrl_ref, br_ref,
                         xo_ref, ym_ref, rt_ref, cnt_ref, seen_ref)


def _outproj_ab(tok, att, h6, conv_w, conv_b, w_out, idx, x, gt, g, sc, sh, wr, br):
    t, d = x.shape
    cw = att.shape[1]
    tm = TOKEN_TILE
    assert tok.seq_ctx % tm == 0 and tok.seq_lat % tm == 0
    halo = tm // V7X_SUBLANES
    n_halo = t // V7X_SUBLANES

    def slab(s):
        return pl.BlockSpec((None, tm, cw), lambda i: (s, i, 0))

    def before(s):
        return pl.BlockSpec((None, V7X_SUBLANES, cw), lambda i: (s, jnp.maximum(i * halo - 1, 0), 0))

    def after(s):
        return pl.BlockSpec((None, V7X_SUBLANES, cw),
                            lambda i: (s, jnp.minimum((i + 1) * halo, n_halo - 1), 0))

    out_shapes, out_specs = _tail_out(t, tm, d)
    return pl.pallas_call(
        functools.partial(_outproj_ab_kernel, tok=tok, tm=tm),
        out_shape=out_shapes,
        grid=(t // tm,),
        in_specs=[pl.BlockSpec((tm, cw), lambda i: (i, 0)), slab(3), slab(4), slab(5),
                  before(4), before(5), after(4), after(5),
                  pl.BlockSpec((None, 3, cw), lambda i: (idx, 0, 0)),
                  pl.BlockSpec((None, 1, cw), lambda i: (idx, 0, 0)),
                  _resident((None, 2 * cw, d), lambda i: (idx, 0, 0))] + _tail_specs(tok, tm, d),
        out_specs=out_specs,
        scratch_shapes=[pltpu.VMEM((2 * cw, d), BF16)] + _tail_scratch(d),
        compiler_params=_params(1),
        name="outproj_attn_conv",
    )(att, h6, h6, h6, h6, h6, h6, h6, conv_w, conv_b.reshape(conv_b.shape[0], 1, cw), w_out,
      x, gt, g, sc, sh, wr, br)


def _outproj_sgu_kernel(u0_ref, u1_ref, v0_ref, v1_ref, gs_ref, ws_ref, bs_ref, w_ref,
                        x_ref, gt_ref, g_ref, sc_ref, sh_ref, wr_ref, br_ref,
                        xo_ref, ym_ref, rt_ref, cnt_ref,
                        wb_ref, wsb_ref, gated_ref, wrh_ref, wrl_ref, seen_ref, *, tm):
    @pl.when(pl.program_id(0) == 0)
    def _():
        wb_ref[...] = w_ref[...].astype(BF16)
        wsb_ref[...] = ws_ref[...].astype(BF16)
        _init_router(wr_ref, wrh_ref, wrl_ref, seen_ref)

    half = u0_ref.shape[1]
    d = 2 * half
    v0 = v0_ref[...]
    v1 = v1_ref[...]
    ms = (jnp.sum(v0 * v0, axis=-1, keepdims=True) + jnp.sum(v1 * v1, axis=-1, keepdims=True)) / d
    inv = lax.rsqrt(ms + EPS)
    gd = SGU_GROUP_DIM
    for grp in range(d // gd):
        u_ref, v_ref = (u0_ref, v0_ref) if grp * gd < half else (u1_ref, v1_ref)
        loc = (grp * gd) % half
        vg = (v_ref[:, loc:loc + gd] * inv) * gs_ref[:, grp * gd:(grp + 1) * gd]
        for c in range(tm // CHUNK):
            rows = slice(c * CHUNK, (c + 1) * CHUNK)
            sp = _dot(wsb_ref[grp], vg[rows].astype(BF16)) + bs_ref[grp]
            gated_ref[rows, grp * gd:(grp + 1) * gd] = (u_ref[rows, loc:loc + gd] * sp).astype(BF16)
    acc = _dot(gated_ref[...], wb_ref[...])
    _residual_norm_route(acc, x_ref, gt_ref, g_ref, sc_ref, sh_ref, wrh_ref, wrl_ref, br_ref,
                         xo_ref, ym_ref, rt_ref, cnt_ref, seen_ref)


def _outproj_sgu(tok, uv4, g_sgu, w_s, b_s, w_out, idx, x, gt, g, sc, sh, wr, br):
    t, d = x.shape
    half = d // 2
    tm = TOKEN_TILE
    assert tok.seq_ctx % tm == 0 and tok.seq_lat % tm == 0 and tm % CHUNK == 0
    groups = d // SGU_GROUP_DIM
    bs = jnp.broadcast_to(b_s[idx][:, :, None], (groups, CHUNK, SGU_GROUP_DIM))

    def slab(s):
        return pl.BlockSpec((None, tm, half), lambda i: (s, i, 0))

    out_shapes, out_specs = _tail_out(t, tm, d)
    return pl.pallas_call(
        functools.partial(_outproj_sgu_kernel, tm=tm),
        out_shape=out_shapes,
        grid=(t // tm,),
        in_specs=[slab(0), slab(1), slab(2), slab(3),
                  pl.BlockSpec((None, 1, d), lambda i: (idx, 0, 0)),
                  _resident((None, groups, CHUNK, CHUNK), lambda i: (idx, 0, 0, 0)),
                  _resident((groups, CHUNK, SGU_GROUP_DIM), lambda i: (0, 0, 0)),
                  _resident((None, d, d), lambda i: (idx, 0, 0))] + _tail_specs(tok, tm, d),
        out_specs=out_specs,
        scratch_shapes=[pltpu.VMEM((d, d), BF16), pltpu.VMEM((groups, CHUNK, CHUNK), BF16),
                        pltpu.VMEM((tm, d), BF16)] + _tail_scratch(d),
        compiler_params=_params(1),
        name="outproj_gmlp",
    )(uv4, uv4, uv4, uv4, g_sgu.reshape(g_sgu.shape[0], 1, d), w_s, bs, w_out,
      x, gt, g, sc, sh, wr, br)


def _route_metadata(route, counts, n_experts, tm):
    t = route.shape[0]
    experts = jnp.arange(n_experts, dtype=jnp.int32)
    counts = counts[0, :n_experts].astype(jnp.int32)
    padded = ((counts + tm - 1) // tm) * tm
    ends = jnp.cumsum(padded)
    starts = ends - padded
    e_idx = route[:, TOP_K:2 * TOP_K].astype(jnp.int32)
    rank = route[:, 2 * TOP_K:3 * TOP_K].astype(jnp.int32)
    start_of = jnp.sum(jnp.where(e_idx[:, :, None] == experts, starts, 0), axis=-1)
    pos = (start_of + rank).reshape(-1)
    m_tiles = (t * TOP_K) // tm + n_experts
    tile_start = jnp.arange(m_tiles, dtype=jnp.int32) * tm
    tile_expert = jnp.sum((ends[None, :] <= tile_start[:, None]).astype(jnp.int32), axis=1)
    tile_expert = jnp.minimum(tile_expert, n_experts - 1)
    later = (experts[None, :] > experts[:, None]) & (counts[None, :] > 0)
    next_of = jnp.min(jnp.where(later, experts[None, :], n_experts), axis=1)
    next_of = jnp.where(next_of == n_experts, -1, next_of)
    tile_next = jnp.sum(jnp.where(tile_expert[:, None] == experts, next_of, 0), axis=1)
    tiles = jnp.concatenate([tile_expert, ends[-1:] // tm, tile_next]).astype(jnp.int32)
    src = jnp.zeros((m_tiles * tm,), jnp.int32).at[pos].set(
        jnp.arange(t * TOP_K, dtype=jnp.int32) // TOP_K)
    return pos, tiles, src


def _used_tile(tiles_ref, m, n_tiles):
    return jnp.minimum(m, tiles_ref[n_tiles] - 1)


def _gather_kernel(tiles_ref, src_ref, ym_hbm, o_ref, buf, sem, *, tm, n_tiles):
    i = pl.program_id(0)
    n_used = tiles_ref[n_tiles]

    def issue(tile, slot):
        base = tile * tm
        for r in range(tm):
            pltpu.make_async_copy(ym_hbm.at[pl.ds(src_ref[base + r], 1)], buf.at[slot, pl.ds(r, 1)],
                                  sem.at[slot]).start()

    @pl.when(i == 0)
    def _():
        issue(0, 0)

    @pl.when(i + 1 < n_used)
    def _():
        issue(i + 1, (i + 1) % 2)

    @pl.when(i < n_used)
    def _():
        slot = i % 2
        pltpu.make_async_copy(ym_hbm.at[pl.ds(0, tm)], buf.at[slot], sem.at[slot]).wait()
        o_ref[...] = buf[slot].astype(o_ref.dtype)

    @pl.when(i >= n_used)
    def _():
        o_ref[...] = jnp.zeros(o_ref.shape, o_ref.dtype)


def _dispatch_gather(tiles, src, ym, tm):
    t, d = ym.shape
    n_tiles = src.shape[0] // tm
    return pl.pallas_call(
        functools.partial(_gather_kernel, tm=tm, n_tiles=n_tiles),
        out_shape=jax.ShapeDtypeStruct((n_tiles * tm, d), BF16),
        grid_spec=pltpu.PrefetchScalarGridSpec(
            num_scalar_prefetch=2,
            grid=(n_tiles,),
            in_specs=[pl.BlockSpec(memory_space=pl.ANY)],
            out_specs=pl.BlockSpec((tm, d), lambda i, tiles, src: (i, 0)),
            scratch_shapes=[pltpu.VMEM((2, tm, d), F32), pltpu.SemaphoreType.DMA((2,))],
        ),
        compiler_params=_params(1),
        name="moe_dispatch_gather",
    )(tiles, src, ym)


def _expert_changed(tiles_ref, m):
    return jnp.logical_or(m == 0, tiles_ref[m] != tiles_ref[jnp.maximum(m - 1, 0)])


def _expert_weights(tiles_ref, n_tiles, m, copies, group_ref, on_arrival):
    @pl.when(m == 0)
    def _():
        group_ref[0] = 0
        for c in copies(tiles_ref[0], 0):
            c.start()

    @pl.when(_expert_changed(tiles_ref, m))
    def _():
        slot = group_ref[0] % 2
        for c in copies(tiles_ref[m], slot):
            c.wait()
        nxt = tiles_ref[n_tiles + 1 + m]

        @pl.when(nxt >= 0)
        def _():
            for c in copies(nxt, 1 - slot):
                c.start()

        on_arrival(slot)
        group_ref[0] = group_ref[0] + 1


def _gmm1_kernel(tiles_ref, x_ref, w_hbm, bg_ref, bu_ref, o_ref, stage, wgb_ref, wub_ref, sem, group_ref,
                 *, n_tiles, layer, nf, fn):
    n = pl.program_id(0)
    m = pl.program_id(1)

    def copies(expert, slot):
        cols = [pl.ds(pl.multiple_of(n * fn, fn), fn), pl.ds(pl.multiple_of((nf + n) * fn, fn), fn)]
        return [pltpu.make_async_copy(w_hbm.at[layer, expert, :, cols[part]], stage.at[slot, part],
                                      sem.at[slot]) for part in range(2)]

    def on_arrival(slot):
        wgb_ref[...] = stage[slot, 0].astype(BF16)
        wub_ref[...] = stage[slot, 1].astype(BF16)

    @pl.when(m < tiles_ref[n_tiles])
    def _():
        _expert_weights(tiles_ref, n_tiles, m, copies, group_ref, on_arrival)
        x = x_ref[...]
        gate = jnp.minimum(_dot(x, wgb_ref[...]) + bg_ref[...], SWIGLU_LIMIT)
        up = jnp.clip(_dot(x, wub_ref[...]) + bu_ref[...], -SWIGLU_LIMIT, SWIGLU_LIMIT)
        o_ref[...] = (gate * jax.nn.sigmoid(SWIGLU_ALPHA * gate) * (up + 1.0)).astype(o_ref.dtype)

    @pl.when(m >= tiles_ref[n_tiles])
    def _():
        o_ref[...] = jnp.zeros(o_ref.shape, o_ref.dtype)


def _gmm1(tiles, xs, w1, b1, layer, tm):
    m_rows, d = xs.shape
    n_experts, f = w1.shape[1], w1.shape[3] // 2
    fn = min(COL_TILE, f)
    nf = f // fn
    n_tiles = m_rows // tm
    b1r = b1.reshape(b1.shape[0], n_experts, 1, 2 * f)

    def expert(m, tiles):
        return tiles[_used_tile(tiles, m, n_tiles)]

    return pl.pallas_call(
        functools.partial(_gmm1_kernel, n_tiles=n_tiles, layer=layer, nf=nf, fn=fn),
        out_shape=jax.ShapeDtypeStruct((m_rows, f), BF16),
        grid_spec=pltpu.PrefetchScalarGridSpec(
            num_scalar_prefetch=1,
            grid=(nf, n_tiles),
            in_specs=[pl.BlockSpec((tm, d), lambda n, m, tl: (_used_tile(tl, m, n_tiles), 0)),
                      pl.BlockSpec(memory_space=pl.ANY),
                      pl.BlockSpec((None, None, 1, fn), lambda n, m, tl: (layer, expert(m, tl), 0, n)),
                      pl.BlockSpec((None, None, 1, fn), lambda n, m, tl: (layer, expert(m, tl), 0, nf + n))],
            out_specs=pl.BlockSpec((tm, fn), lambda n, m, tl: (m, n)),
            scratch_shapes=[pltpu.VMEM((2, 2, d, fn), F32), pltpu.VMEM((d, fn), BF16),
                            pltpu.VMEM((d, fn), BF16), pltpu.SemaphoreType.DMA((2,)),
                            pltpu.SMEM((1,), jnp.int32)],
        ),
        compiler_params=_params(2),
        name="moe_expert_up",
    )(tiles, xs, w1, b1r, b1r)


def _gmm2_kernel(tiles_ref, a_ref, w_hbm, b_ref, o_ref, stage, wb_ref, sem, group_ref,
                 *, n_tiles, layer, dn):
    n = pl.program_id(0)
    m = pl.program_id(1)

    def copies(expert, slot):
        cols = pl.ds(pl.multiple_of(n * dn, dn), dn)
        return [pltpu.make_async_copy(w_hbm.at[layer, expert, :, cols], stage.at[slot], sem.at[slot])]

    def on_arrival(slot):
        wb_ref[...] = stage[slot].astype(BF16)

    @pl.when(m < tiles_ref[n_tiles])
    def _():
        _expert_weights(tiles_ref, n_tiles, m, copies, group_ref, on_arrival)
        o_ref[...] = _dot(a_ref[...], wb_ref[...]) + b_ref[...]

    @pl.when(m >= tiles_ref[n_tiles])
    def _():
        o_ref[...] = jnp.zeros(o_ref.shape, o_ref.dtype)


def _gmm2(tiles, a, w2, b2, layer, tm):
    m_rows, f = a.shape
    n_experts, d = w2.shape[1], w2.shape[3]
    dn = min(COL_TILE, d)
    n_tiles = m_rows // tm
    b2r = b2.reshape(b2.shape[0], n_experts, 1, d)

    def expert(m, tiles):
        return tiles[_used_tile(tiles, m, n_tiles)]

    return pl.pallas_call(
        functools.partial(_gmm2_kernel, n_tiles=n_tiles, layer=layer, dn=dn),
        out_shape=jax.ShapeDtypeStruct((m_rows, d), F32),
        grid_spec=pltpu.PrefetchScalarGridSpec(
            num_scalar_prefetch=1,
            grid=(d // dn, n_tiles),
            in_specs=[pl.BlockSpec((tm, f), lambda n, m, tl: (_used_tile(tl, m, n_tiles), 0)),
                      pl.BlockSpec(memory_space=pl.ANY),
                      pl.BlockSpec((None, None, 1, dn), lambda n, m, tl: (layer, expert(m, tl), 0, n))],
            out_specs=pl.BlockSpec((tm, dn), lambda n, m, tl: (m, n)),
            scratch_shapes=[pltpu.VMEM((2, f, dn), F32), pltpu.VMEM((f, dn), BF16),
                            pltpu.SemaphoreType.DMA((2,)), pltpu.SMEM((1,), jnp.int32)],
        ),
        compiler_params=_params(2),
        name="moe_expert_down",
    )(tiles, a, w2, b2r)


def _combine_kernel(pos_ref, y_hbm, x_ref, rt_ref, gt_ref, g_ref, sc_ref, sh_ref, *rest,
                    tmc, n_tiles, final):
    if final:
        out_ref, buf, sem = rest
    else:
        xo_ref, out_ref, buf, sem = rest
    i = pl.program_id(0)

    def issue(tile, slot):
        base = tile * (tmc * TOP_K)
        for t in range(tmc):
            for k in range(TOP_K):
                p = pos_ref[base + t * TOP_K + k]
                pltpu.make_async_copy(y_hbm.at[pl.ds(p, 1)], buf.at[slot, pl.ds(k * tmc + t, 1)],
                                      sem.at[slot]).start()

    @pl.when(i == 0)
    def _():
        issue(0, 0)

    @pl.when(i + 1 < n_tiles)
    def _():
        issue(i + 1, (i + 1) % 2)

    slot = i % 2
    pltpu.make_async_copy(y_hbm.at[pl.ds(0, TOP_K * tmc)], buf.at[slot], sem.at[slot]).wait()
    f = rt_ref[:, 0:1] * buf[slot, pl.ds(0, tmc), :]
    for k in range(1, TOP_K):
        f = f + rt_ref[:, k:k + 1] * buf[slot, pl.ds(k * tmc, tmc), :]
    xn = x_ref[...] + gt_ref[...] * f
    if final:
        y = xn * lax.rsqrt(jnp.mean(xn * xn, axis=-1, keepdims=True) + EPS)
        out_ref[...] = y * g_ref[...]
    else:
        xo_ref[...] = xn
        out_ref[...] = _rms_mod(xn, g_ref[...], sc_ref[...], sh_ref[...]).astype(out_ref.dtype)


def _combine(tok, pos, y, x, route, gt, g, sc, sh, final):
    t, d = x.shape
    tmc = COMBINE_TILE
    n_tiles = t // tmc
    tile = pl.BlockSpec((tmc, d), lambda i, pos: (i, 0))
    if final:
        out_shape = jax.ShapeDtypeStruct((t, d), F32)
        out_specs = tile
    else:
        out_shape = [jax.ShapeDtypeStruct((t, d), F32), jax.ShapeDtypeStruct((t, d), BF16)]
        out_specs = [tile, tile]
    return pl.pallas_call(
        functools.partial(_combine_kernel, tmc=tmc, n_tiles=n_tiles, final=final),
        out_shape=out_shape,
        grid_spec=pltpu.PrefetchScalarGridSpec(
            num_scalar_prefetch=1,
            grid=(n_tiles,),
            in_specs=[pl.BlockSpec(memory_space=pl.ANY), tile,
                      pl.BlockSpec((tmc, V7X_LANES), lambda i, pos: (i, 0)),
                      tok.row_spec(tmc, d), _vec_spec(d), tok.row_spec(tmc, d), tok.row_spec(tmc, d)],
            out_specs=out_specs,
            scratch_shapes=[pltpu.VMEM((2, TOP_K * tmc, d), F32), pltpu.SemaphoreType.DMA((2,))],
        ),
        compiler_params=_params(1),
        name="moe_combine",
    )(pos, y, x, route, gt, g, sc, sh)


def kernel(x_prompt, x_sample, cache_k, cache_v, c, c_ctx, w_ada, b_ada, g_mix, g_ffn, g_final,
           w_in_ab, conv_w, conv_b, rpb, w_out_ab, w_in_c, g_sgu, w_s, b_s, w_out_c,
           w_router, b_router, w1, b1, w2, b2):
    n_b, seq, d = x_prompt.shape
    n_db, dseq, _ = x_sample.shape
    depth = w_ada.shape[0]
    n_experts = w_router.shape[-1]
    heads, head_dim = cache_k.shape[3], cache_k.shape[4]
    na_width = heads * head_dim
    assert head_dim == NA_HEAD_DIM and w_in_ab.shape[-1] == 6 * na_width and d == 2 * na_width
    tok = _Tokens(n_b * seq, seq, n_db * dseq, dseq)

    x = jnp.concatenate([x_prompt.reshape(tok.n_ctx, d), x_sample.reshape(tok.n_lat, d)], axis=0)

    n_rows = 1 + n_db
    pad_rows = -n_rows % (2 * V7X_SUBLANES)
    cvec = jnp.concatenate([c_ctx[None, :], c, jnp.zeros((pad_rows, d), F32)], axis=0)
    mod = _modulation(cvec, w_ada, b_ada)[:, :n_rows]
    mod = mod.reshape(depth, n_rows, 6, 1, d).transpose(0, 2, 1, 3, 4)

    wr_pad = jnp.pad(w_router, ((0, 0), (0, 0), (0, V7X_LANES - n_experts)))
    br_pad = jnp.pad(b_router, ((0, 0), (0, V7X_LANES - n_experts)), constant_values=MASKED)
    rows = dseq // GRID_W

    xm = _first_norm(tok, x, g_mix[0][None], mod[0, 1], mod[0, 0])
    new_k, new_v = [], []
    out = None
    for l in range(depth):
        sh1, sc1, gt1, sh2, sc2, gt2 = (mod[l, s] for s in range(6))
        tail = (x, gt1, g_ffn[l][None], sc2, sh2, wr_pad[l], br_pad[l][None])
        if l % 2 == 0:
            i = l // 2
            h6 = _projection(xm, w_in_ab, i, na_width)
            att_c = _context_attention(h6, n_b, seq, na_width)
            att_l = _neighbourhood_attention(h6, cache_k, cache_v, i, _nbr_bias_blocks(rpb[i], rows),
                                             tok, n_db, na_width)
            att = jnp.concatenate([att_c, att_l], axis=0)
            x, ym, route, counts = _outproj_ab(tok, att, h6, conv_w, conv_b, w_out_ab, i, *tail)
            new_k.append(h6[1, :tok.n_ctx].reshape(n_b, seq, heads, head_dim))
            new_v.append(h6[2, :tok.n_ctx].reshape(n_b, seq, heads, head_dim))
        else:
            j = l // 2
            uv4 = _projection(xm, w_in_c, j, d // 2)
            x, ym, route, counts = _outproj_sgu(tok, uv4, g_sgu, w_s, b_s, w_out_c, j, *tail)
        pos, tiles, src = _route_metadata(route, counts, n_experts, TOKEN_TILE)
        xs = _dispatch_gather(tiles, src, ym, TOKEN_TILE)
        a = _gmm1(tiles, xs, w1, b1, l, TOKEN_TILE)
        y = _gmm2(tiles, a, w2, b2, l, TOKEN_TILE)
        if l + 1 < depth:
            x, xm = _combine(tok, pos, y, x, route, gt2, g_mix[l + 1][None],
                             mod[l + 1, 1], mod[l + 1, 0], final=False)
        else:
            out = _combine(tok, pos, y, x, route, gt2, g_final[None], gt2, gt2, final=True)

    y_prompt = out[:tok.n_ctx].reshape(n_b, seq, d)
    y_sample = out[tok.n_ctx:].reshape(n_db, dseq, d)
    return (y_prompt, y_sample, jnp.stack(new_k, axis=1), jnp.stack(new_v, axis=1))
```

```python
import functools

import jax
import jax.numpy as jnp
from jax import lax
from jax.experimental import pallas as pl
from jax.experimental.pallas import tpu as pltpu

GRID_W = 64
NA_KH = 8
NA_KW = 16
NA_HEAD_DIM = 64
CHUNK = 128
SGU_GROUP_DIM = 128
TOP_K = 4
SWIGLU_LIMIT = 7.0
SWIGLU_ALPHA = 1.702
EPS = 1e-6

V7X_LANES = 128
V7X_SUBLANES = 8
V7X_VMEM_LIMIT_BYTES = 56 * 1024 * 1024

TOKEN_TILE = 256
PROJ_TILE = 1024
COMBINE_TILE = 128
COL_TILE = 1024
DOWN_COL_TILE = 2048
MASKED = -1e30

F32 = jnp.float32
BF16 = jnp.bfloat16


def _params(n_axes):
    return pltpu.CompilerParams(
        dimension_semantics=("arbitrary",) * n_axes,
        vmem_limit_bytes=V7X_VMEM_LIMIT_BYTES,
    )


def _dot(a, b):
    return jnp.dot(a, b, preferred_element_type=F32)


def _dot_nt(a, b):
    return lax.dot_general(a, b, (((1,), (1,)), ((), ())), preferred_element_type=F32)


def _split_bf16(a):
    hi = a.astype(BF16)
    lo = (a - hi.astype(F32)).astype(BF16)
    return hi, lo


def _rms_mod(x, g, sc, sh):
    y = x * lax.rsqrt(jnp.mean(x * x, axis=-1, keepdims=True) + EPS)
    return (y * g) * (1.0 + sc) + sh


def _mod_kernel(c_ref, w_ref, b_ref, o_ref):
    c = c_ref[...]
    s_hi, s_lo = _split_bf16(c * jax.nn.sigmoid(c))
    w_hi, w_lo = _split_bf16(w_ref[...])
    o_ref[...] = _dot(s_hi, w_hi) + _dot(s_hi, w_lo) + _dot(s_lo, w_hi) + b_ref[...]


def _modulation(cvec, w_ada, b_ada):
    depth, d, n = w_ada.shape
    rows = cvec.shape[0]
    tn = min(512, n)
    return pl.pallas_call(
        _mod_kernel,
        out_shape=jax.ShapeDtypeStruct((depth, rows, n), F32),
        grid=(depth, n // tn),
        in_specs=[
            pl.BlockSpec((rows, d), lambda l, j: (0, 0)),
            pl.BlockSpec((None, d, tn), lambda l, j: (l, 0, j)),
            pl.BlockSpec((None, 1, tn), lambda l, j: (l, 0, j)),
        ],
        out_specs=pl.BlockSpec((None, rows, tn), lambda l, j: (l, 0, j)),
        compiler_params=_params(2),
        name="modulation",
    )(cvec, w_ada, b_ada.reshape(depth, 1, n))


class _Tokens:
    def __init__(self, n_ctx, seq_ctx, n_lat, seq_lat):
        self.n_ctx, self.seq_ctx, self.n_lat, self.seq_lat = n_ctx, seq_ctx, n_lat, seq_lat
        self.total = n_ctx + n_lat

    def mod_row(self, tile, tm):
        nct = self.n_ctx // tm
        per = self.seq_lat // tm
        return jnp.where(tile < nct, 0, 1 + (tile - nct) // per)

    def row_spec(self, tm, d):
        return pl.BlockSpec((None, 1, d), lambda i, *_: (self.mod_row(i, tm), 0, 0))


def _vec_spec(d):
    return pl.BlockSpec((1, d), lambda i, *_: (0, 0))


def _norm_kernel(x_ref, g_ref, sc_ref, sh_ref, o_ref):
    o_ref[...] = _rms_mod(x_ref[...], g_ref[...], sc_ref[...], sh_ref[...]).astype(o_ref.dtype)


def _first_norm(tok, x, g, sc, sh):
    t, d = x.shape
    tm = TOKEN_TILE
    return pl.pallas_call(
        _norm_kernel,
        out_shape=jax.ShapeDtypeStruct((t, d), BF16),
        grid=(t // tm,),
        in_specs=[pl.BlockSpec((tm, d), lambda i: (i, 0)), _vec_spec(d),
                  tok.row_spec(tm, d), tok.row_spec(tm, d)],
        out_specs=pl.BlockSpec((tm, d), lambda i: (i, 0)),
        compiler_params=_params(1),
        name="first_norm",
    )(x, g, sc, sh)


def _proj_kernel(x_ref, w_ref, o_ref, wb_ref):
    @pl.when(pl.program_id(1) == 0)
    def _():
        wb_ref[...] = w_ref[...].astype(BF16)

    o_ref[...] = _dot(x_ref[...], wb_ref[...])


def _projection(xm, w, idx, cw):
    t, d = xm.shape
    n = w.shape[-1]
    tm = PROJ_TILE
    return pl.pallas_call(
        _proj_kernel,
        out_shape=jax.ShapeDtypeStruct((n // cw, t, cw), F32),
        grid=(n // cw, t // tm),
        in_specs=[pl.BlockSpec((tm, d), lambda j, i: (i, 0)),
                  pl.BlockSpec((None, d, cw), lambda j, i: (idx, 0, j))],
        out_specs=pl.BlockSpec((None, tm, cw), lambda j, i: (j, i, 0)),
        scratch_shapes=[pltpu.VMEM((d, cw), BF16)],
        compiler_params=_params(2),
        name="projection",
    )(xm, w)


def _ctx_attn_kernel(q_ref, k_ref, v_ref, o_ref, *, heads):
    scale = NA_HEAD_DIM ** -0.5
    for h in range(heads):
        sl = slice(h * NA_HEAD_DIM, (h + 1) * NA_HEAD_DIM)
        q = q_ref[:, sl].astype(BF16)
        k = k_ref[:, sl].astype(BF16)
        v = v_ref[:, sl].astype(BF16)
        s = _dot_nt(q, k) * scale
        p = jnp.exp(s - jnp.max(s, axis=-1, keepdims=True))
        p = p / jnp.sum(p, axis=-1, keepdims=True)
        o_ref[:, sl] = _dot(p.astype(BF16), v).astype(o_ref.dtype)


def _context_attention(h6, n_batch, seq, cw):
    return pl.pallas_call(
        functools.partial(_ctx_attn_kernel, heads=cw // NA_HEAD_DIM),
        out_shape=jax.ShapeDtypeStruct((n_batch * seq, cw), BF16),
        grid=(n_batch,),
        in_specs=[pl.BlockSpec((None, seq, cw), lambda b: (0, b, 0)),
                  pl.BlockSpec((None, seq, cw), lambda b: (1, b, 0)),
                  pl.BlockSpec((None, seq, cw), lambda b: (2, b, 0))],
        out_specs=pl.BlockSpec((seq, cw), lambda b: (b, 0)),
        compiler_params=_params(1),
        name="context_attention",
    )(h6, h6, h6)


NBR_QUERY_ROWS = 8
NBR_KEY_ROWS = NBR_QUERY_ROWS + NA_KH // 2
HEADS_PER_STEP = V7X_LANES // NA_HEAD_DIM


def _nbr_key_start(blk, rows):
    if isinstance(blk, int):
        return min(max(blk * NBR_QUERY_ROWS - NA_KH // 2, 0), rows - NBR_KEY_ROWS)
    return jnp.clip(blk * NBR_QUERY_ROWS - NA_KH // 2, 0, rows - NBR_KEY_ROWS)


def _nbr_bias_blocks(rpb, rows):
    heads = rpb.shape[0]
    n_blk = rows // NBR_QUERY_ROWS
    ext = jnp.pad(rpb.astype(F32), ((0, 0), (0, 0), (GRID_W - NA_KW, GRID_W - NA_KW)), mode="edge")
    tab = jnp.stack([ext[:, :, GRID_W - 1 - qc:2 * GRID_W - 1 - qc] for qc in range(GRID_W)], axis=2)
    cols = jnp.arange(GRID_W)
    cs = jnp.clip(cols - NA_KW // 2, 0, GRID_W - NA_KW)
    col_ok = (cols[None, :] >= cs[:, None]) & (cols[None, :] < cs[:, None] + NA_KW)
    tab = jnp.where(col_ok[None, None], tab, MASKED)
    masked = jnp.full((heads, GRID_W, GRID_W), MASKED, F32)
    blocks = []
    for blk in range(n_blk):
        k0 = _nbr_key_start(blk, rows)
        q_rows = []
        for qr in range(blk * NBR_QUERY_ROWS, (blk + 1) * NBR_QUERY_ROWS):
            rs = min(max(qr - NA_KH // 2, 0), rows - NA_KH)
            q_rows.append(jnp.concatenate(
                [tab[:, kr - qr + NA_KH - 1] if rs <= kr < rs + NA_KH else masked
                 for kr in range(k0, k0 + NBR_KEY_ROWS)], axis=-1))
        blocks.append(jnp.concatenate(q_rows, axis=1))
    return jnp.stack(blocks, axis=1)


def _nbr_attn_kernel(q_ref, k_ref, v_ref, kc_ref, vc_ref, bias_ref, o_ref, *, rows):
    scale = NA_HEAD_DIM ** -0.5
    start = pl.multiple_of(_nbr_key_start(pl.program_id(0), rows) * GRID_W, GRID_W)
    win = pl.ds(start, NBR_KEY_ROWS * GRID_W)
    for h in range(HEADS_PER_STEP):
        sl = slice(h * NA_HEAD_DIM, (h + 1) * NA_HEAD_DIM)
        q = (q_ref[:, sl] * scale).astype(BF16)
        s_loc = _dot_nt(q, k_ref[win, sl].astype(BF16)) + bias_ref[h]
        s_ctx = _dot_nt(q, kc_ref[:, sl].astype(BF16))
        m = jnp.maximum(jnp.max(s_loc, axis=-1, keepdims=True), jnp.max(s_ctx, axis=-1, keepdims=True))
        p_loc = jnp.exp(s_loc - m)
        p_ctx = jnp.exp(s_ctx - m)
        inv = 1.0 / (jnp.sum(p_loc, axis=-1, keepdims=True) + jnp.sum(p_ctx, axis=-1, keepdims=True))
        o = (_dot((p_loc * inv).astype(BF16), v_ref[win, sl].astype(BF16))
             + _dot((p_ctx * inv).astype(BF16), vc_ref[:, sl].astype(BF16)))
        o_ref[:, sl] = o.astype(o_ref.dtype)


def _neighbourhood_attention(h6, cache_k, cache_v, idx, bias, tok, n_batch, cw):
    rows = tok.seq_lat // GRID_W
    past = cache_k.shape[2]
    heads = cw // NA_HEAD_DIM
    ck = cache_k.reshape(cache_k.shape[0], cache_k.shape[1], past, cw)
    cv = cache_v.reshape(cache_v.shape[0], cache_v.shape[1], past, cw)
    assert tok.n_ctx % tok.seq_lat == 0 and heads % HEADS_PER_STEP == 0
    assert rows % NBR_QUERY_ROWS == 0 and rows >= NBR_KEY_ROWS
    n_blk = rows // NBR_QUERY_ROWS
    for blk in range(n_blk):
        k0 = _nbr_key_start(blk, rows)
        for r in range(blk * NBR_QUERY_ROWS, (blk + 1) * NBR_QUERY_ROWS):
            rs = min(max(r - NA_KH // 2, 0), rows - NA_KH)
            assert k0 <= rs and rs + NA_KH <= k0 + NBR_KEY_ROWS
    tq = NBR_QUERY_ROWS * GRID_W
    q0 = tok.n_ctx // tq
    s0 = tok.n_ctx // tok.seq_lat
    lanes = V7X_LANES

    return pl.pallas_call(
        functools.partial(_nbr_attn_kernel, rows=rows),
        out_shape=jax.ShapeDtypeStruct((tok.n_lat, cw), BF16),
        grid=(n_blk, heads // HEADS_PER_STEP, n_batch),
        in_specs=[pl.BlockSpec((None, tq, lanes), lambda k, h, b: (0, q0 + b * n_blk + k, h)),
                  pl.BlockSpec((None, tok.seq_lat, lanes), lambda k, h, b: (1, s0 + b, h)),
                  pl.BlockSpec((None, tok.seq_lat, lanes), lambda k, h, b: (2, s0 + b, h)),
                  pl.BlockSpec((None, None, past, lanes), lambda k, h, b: (b, idx, 0, h)),
                  pl.BlockSpec((None, None, past, lanes), lambda k, h, b: (b, idx, 0, h)),
                  pl.BlockSpec((HEADS_PER_STEP, None, tq, NBR_KEY_ROWS * GRID_W),
                               lambda k, h, b: (h, k, 0, 0))],
        out_specs=pl.BlockSpec((tq, lanes), lambda k, h, b: (b * n_blk + k, h)),
        compiler_params=_params(3),
        name="neighbourhood_attention",
    )(h6, h6, h6, ck, cv, bias)


def _route_top_k(logits, seen_ref):
    tm = logits.shape[0]
    lane = lax.broadcasted_iota(jnp.int32, logits.shape, 1)
    l = logits
    vals, picks = [], []
    for _ in range(TOP_K):
        m = jnp.max(l, axis=-1, keepdims=True)
        idx = jnp.min(jnp.where(l == m, lane, V7X_LANES), axis=-1, keepdims=True)
        vals.append(m)
        picks.append(lane == idx)
        l = jnp.where(picks[-1], -jnp.inf, l)
    e = [jnp.exp(v - vals[0]) for v in vals]
    den = e[0]
    picked = picks[0]
    for k in range(1, TOP_K):
        den = den + e[k]
        picked = jnp.logical_or(picked, picks[k])
    onehot = picked.astype(F32)
    earlier = (lax.broadcasted_iota(jnp.int32, (tm, tm), 1)
               < lax.broadcasted_iota(jnp.int32, (tm, tm), 0)).astype(BF16)
    before = seen_ref[0:1, :] + _dot(earlier, onehot.astype(BF16))
    lane_f = lane.astype(F32)
    route = jnp.zeros(logits.shape, F32)
    for k in range(TOP_K):
        idx_f = jnp.sum(jnp.where(picks[k], lane_f, 0.0), axis=-1, keepdims=True)
        rank = jnp.sum(jnp.where(picks[k], before, 0.0), axis=-1, keepdims=True)
        route = jnp.where(lane == k, e[k] / den, route)
        route = jnp.where(lane == TOP_K + k, idx_f, route)
        route = jnp.where(lane == 2 * TOP_K + k, rank, route)
    seen = seen_ref[0:1, :] + jnp.sum(onehot, axis=0, keepdims=True)
    seen_ref[...] = jnp.broadcast_to(seen, seen_ref.shape)
    return route


def _residual_norm_route(acc, x_ref, gt_ref, g_ref, sc_ref, sh_ref, wrh_ref, wrl_ref, br_ref,
                         xo_ref, ym_ref, rt_ref, cnt_ref, seen_ref):
    xn = x_ref[...] + gt_ref[...] * acc
    xo_ref[...] = xn
    ym = _rms_mod(xn, g_ref[...], sc_ref[...], sh_ref[...])
    ym_ref[...] = ym
    hi, lo = _split_bf16(ym)
    logits = _dot(hi, wrh_ref[...]) + _dot(hi, wrl_ref[...]) + _dot(lo, wrh_ref[...]) + br_ref[...]
    rt_ref[...] = _route_top_k(logits, seen_ref)
    cnt_ref[...] = seen_ref[...]


def _init_router(wr_ref, wrh_ref, wrl_ref, seen_ref):
    hi, lo = _split_bf16(wr_ref[...])
    wrh_ref[...] = hi
    wrl_ref[...] = lo
    seen_ref[...] = jnp.zeros(seen_ref.shape, F32)


def _resident(shape, index):
    return pl.BlockSpec(shape, index, pipeline_mode=pl.Buffered(1))


def _tail_specs(tok, tm, d):
    return [pl.BlockSpec((tm, d), lambda i: (i, 0)), tok.row_spec(tm, d), _vec_spec(d),
            tok.row_spec(tm, d), tok.row_spec(tm, d),
            _resident((d, V7X_LANES), lambda i: (0, 0)), _vec_spec(V7X_LANES)]


def _tail_out(t, tm, d):
    shapes = [jax.ShapeDtypeStruct((t, d), F32), jax.ShapeDtypeStruct((t, d), F32),
              jax.ShapeDtypeStruct((t, V7X_LANES), F32),
              jax.ShapeDtypeStruct((V7X_SUBLANES, V7X_LANES), F32)]
    specs = [pl.BlockSpec((tm, d), lambda i: (i, 0)), pl.BlockSpec((tm, d), lambda i: (i, 0)),
             pl.BlockSpec((tm, V7X_LANES), lambda i: (i, 0)),
             pl.BlockSpec((V7X_SUBLANES, V7X_LANES), lambda i: (0, 0))]
    return shapes, specs


def _tail_scratch(d):
    return [pltpu.VMEM((d, V7X_LANES), BF16), pltpu.VMEM((d, V7X_LANES), BF16),
            pltpu.VMEM((V7X_SUBLANES, V7X_LANES), F32)]


def _outproj_ab_kernel(att_ref, bg_ref, cg_ref, xs_ref, cgp_ref, xsp_ref, cgn_ref, xsn_ref,
                       cw_ref, cb_ref, w_ref,
                       x_ref, gt_ref, g_ref, sc_ref, sh_ref, wr_ref, br_ref,
                       xo_ref, ym_ref, rt_ref, cnt_ref,
                       wb_ref, wrh_ref, wrl_ref, seen_ref, *, tok, tm):
    i = pl.program_id(0)

    @pl.when(i == 0)
    def _():
        wb_ref[...] = w_ref[...].astype(BF16)
        _init_router(wr_ref, wrh_ref, wrl_ref, seen_ref)

    t0 = i * tm
    is_ctx = t0 < tok.n_ctx
    off = jnp.where(is_ctx, t0 % tok.seq_ctx, (t0 - tok.n_ctx) % tok.seq_lat)
    seq = jnp.where(is_ctx, tok.seq_ctx, tok.seq_lat)
    last = V7X_SUBLANES - 1
    z = cg_ref[...] * xs_ref[...]
    z_before = jnp.where(off > 0, cgp_ref[last:, :] * xsp_ref[last:, :], 0.0)
    z_after = jnp.where(off + tm < seq, cgn_ref[:1, :] * xsn_ref[:1, :], 0.0)
    row = lax.broadcasted_iota(jnp.int32, z.shape, 0)
    z_m1 = jnp.where(row == 0, z_before, pltpu.roll(z, 1, 0))
    z_p1 = jnp.where(row == tm - 1, z_after, pltpu.roll(z, tm - 1, 0))
    conv = z_m1 * cw_ref[0:1, :] + z * cw_ref[1:2, :] + z_p1 * cw_ref[2:3, :] + cb_ref[...]
    conv = (bg_ref[...] * conv).astype(BF16)
    cw = conv.shape[1]
    acc = _dot(att_ref[...], wb_ref[:cw, :]) + _dot(conv, wb_ref[cw:, :])
    _residual_norm_route(acc, x_ref, gt_ref, g_ref, sc_ref, sh_ref, wrh_ref, wrl_ref, br_ref,
                         xo_ref, ym_ref, rt_ref, cnt_ref, seen_ref)


def _outproj_ab(tok, att, h6, conv_w, conv_b, w_out, idx, x, gt, g, sc, sh, wr, br):
    t, d = x.shape
    cw = att.shape[1]
    tm = TOKEN_TILE
    assert tok.seq_ctx % tm == 0 and tok.seq_lat % tm == 0
    halo = tm // V7X_SUBLANES
    n_halo = t // V7X_SUBLANES

    def slab(s):
        return pl.BlockSpec((None, tm, cw), lambda i: (s, i, 0))

    def before(s):
        return pl.BlockSpec((None, V7X_SUBLANES, cw), lambda i: (s, jnp.maximum(i * halo - 1, 0), 0))

    def after(s):
        return pl.BlockSpec((None, V7X_SUBLANES, cw),
                            lambda i: (s, jnp.minimum((i + 1) * halo, n_halo - 1), 0))

    out_shapes, out_specs = _tail_out(t, tm, d)
    return pl.pallas_call(
        functools.partial(_outproj_ab_kernel, tok=tok, tm=tm),
        out_shape=out_shapes,
        grid=(t // tm,),
        in_specs=[pl.BlockSpec((tm, cw), lambda i: (i, 0)), slab(3), slab(4), slab(5),
                  before(4), before(5), after(4), after(5),
                  pl.BlockSpec((None, 3, cw), lambda i: (idx, 0, 0)),
                  pl.BlockSpec((None, 1, cw), lambda i: (idx, 0, 0)),
                  _resident((None, 2 * cw, d), lambda i: (idx, 0, 0))] + _tail_specs(tok, tm, d),
        out_specs=out_specs,
        scratch_shapes=[pltpu.VMEM((2 * cw, d), BF16)] + _tail_scratch(d),
        compiler_params=_params(1),
        name="outproj_attn_conv",
    )(att, h6, h6, h6, h6, h6, h6, h6, conv_w, conv_b.reshape(conv_b.shape[0], 1, cw), w_out,
      x, gt, g, sc, sh, wr, br)


def _outproj_sgu_kernel(u0_ref, u1_ref, v0_ref, v1_ref, gs_ref, ws_ref, bs_ref, w_ref,
                        x_ref, gt_ref, g_ref, sc_ref, sh_ref, wr_ref, br_ref,
                        xo_ref, ym_ref, rt_ref, cnt_ref,
                        wb_ref, wsb_ref, gated_ref, wrh_ref, wrl_ref, seen_ref, *, tm):
    @pl.when(pl.program_id(0) == 0)
    def _():
        wb_ref[...] = w_ref[...].astype(BF16)
        wsb_ref[...] = ws_ref[...].astype(BF16)
        _init_router(wr_ref, wrh_ref, wrl_ref, seen_ref)

    half = u0_ref.shape[1]
    d = 2 * half
    v0 = v0_ref[...]
    v1 = v1_ref[...]
    ms = (jnp.sum(v0 * v0, axis=-1, keepdims=True) + jnp.sum(v1 * v1, axis=-1, keepdims=True)) / d
    inv = lax.rsqrt(ms + EPS)
    gd = SGU_GROUP_DIM
    for grp in range(d // gd):
        u_ref, v_ref = (u0_ref, v0_ref) if grp * gd < half else (u1_ref, v1_ref)
        loc = (grp * gd) % half
        vg = (v_ref[:, loc:loc + gd] * inv) * gs_ref[:, grp * gd:(grp + 1) * gd]
        for c in range(tm // CHUNK):
            rows = slice(c * CHUNK, (c + 1) * CHUNK)
            sp = _dot(wsb_ref[grp], vg[rows].astype(BF16)) + bs_ref[grp]
            gated_ref[rows, grp * gd:(grp + 1) * gd] = (u_ref[rows, loc:loc + gd] * sp).astype(BF16)
    acc = _dot(gated_ref[...], wb_ref[...])
    _residual_norm_route(acc, x_ref, gt_ref, g_ref, sc_ref, sh_ref, wrh_ref, wrl_ref, br_ref,
                         xo_ref, ym_ref, rt_ref, cnt_ref, seen_ref)


def _outproj_sgu(tok, uv4, g_sgu, w_s, b_s, w_out, idx, x, gt, g, sc, sh, wr, br):
    t, d = x.shape
    half = d // 2
    tm = TOKEN_TILE
    assert tok.seq_ctx % tm == 0 and tok.seq_lat % tm == 0 and tm % CHUNK == 0
    groups = d // SGU_GROUP_DIM
    bs = jnp.broadcast_to(b_s[idx][:, :, None], (groups, CHUNK, SGU_GROUP_DIM))

    def slab(s):
        return pl.BlockSpec((None, tm, half), lambda i: (s, i, 0))

    out_shapes, out_specs = _tail_out(t, tm, d)
    return pl.pallas_call(
        functools.partial(_outproj_sgu_kernel, tm=tm),
        out_shape=out_shapes,
        grid=(t // tm,),
        in_specs=[slab(0), slab(1), slab(2), slab(3),
                  pl.BlockSpec((None, 1, d), lambda i: (idx, 0, 0)),
                  _resident((None, groups, CHUNK, CHUNK), lambda i: (idx, 0, 0, 0)),
                  _resident((groups, CHUNK, SGU_GROUP_DIM), lambda i: (0, 0, 0)),
                  _resident((None, d, d), lambda i: (idx, 0, 0))] + _tail_specs(tok, tm, d),
        out_specs=out_specs,
        scratch_shapes=[pltpu.VMEM((d, d), BF16), pltpu.VMEM((groups, CHUNK, CHUNK), BF16),
                        pltpu.VMEM((tm, d), BF16)] + _tail_scratch(d),
        compiler_params=_params(1),
        name="outproj_gmlp",
    )(uv4, uv4, uv4, uv4, g_sgu.reshape(g_sgu.shape[0], 1, d), w_s, bs, w_out,
      x, gt, g, sc, sh, wr, br)


def _route_metadata(route, counts, n_experts, tm):
    t = route.shape[0]
    experts = jnp.arange(n_experts, dtype=jnp.int32)
    counts = counts[0, :n_experts].astype(jnp.int32)
    padded = ((counts + tm - 1) // tm) * tm
    ends = jnp.cumsum(padded)
    starts = ends - padded
    e_idx = route[:, TOP_K:2 * TOP_K].astype(jnp.int32)
    rank = route[:, 2 * TOP_K:3 * TOP_K].astype(jnp.int32)
    start_of = jnp.sum(jnp.where(e_idx[:, :, None] == experts, starts, 0), axis=-1)
    pos = (start_of + rank).reshape(-1)
    m_tiles = (t * TOP_K) // tm + n_experts
    tile_start = jnp.arange(m_tiles, dtype=jnp.int32) * tm
    tile_expert = jnp.sum((ends[None, :] <= tile_start[:, None]).astype(jnp.int32), axis=1)
    tile_expert = jnp.minimum(tile_expert, n_experts - 1)
    later = (experts[None, :] > experts[:, None]) & (counts[None, :] > 0)
    next_of = jnp.min(jnp.where(later, experts[None, :], n_experts), axis=1)
    next_of = jnp.where(next_of == n_experts, -1, next_of)
    tile_next = jnp.sum(jnp.where(tile_expert[:, None] == experts, next_of, 0), axis=1)
    tiles = jnp.concatenate([tile_expert, ends[-1:] // tm, tile_next]).astype(jnp.int32)
    src = jnp.zeros((m_tiles * tm,), jnp.int32).at[pos].set(
        jnp.arange(t * TOP_K, dtype=jnp.int32) // TOP_K)
    return pos, tiles, src


def _used_tile(tiles_ref, m, n_tiles):
    return jnp.minimum(m, tiles_ref[n_tiles] - 1)


def _gather_kernel(tiles_ref, src_ref, ym_hbm, o_ref, buf, sem, *, tm, n_tiles):
    i = pl.program_id(0)
    n_used = tiles_ref[n_tiles]

    def issue(tile, slot):
        base = tile * tm
        for r in range(tm):
            pltpu.make_async_copy(ym_hbm.at[pl.ds(src_ref[base + r], 1)], buf.at[slot, pl.ds(r, 1)],
                                  sem.at[slot]).start()

    @pl.when(i == 0)
    def _():
        issue(0, 0)

    @pl.when(i + 1 < n_used)
    def _():
        issue(i + 1, (i + 1) % 2)

    @pl.when(i < n_used)
    def _():
        slot = i % 2
        pltpu.make_async_copy(ym_hbm.at[pl.ds(0, tm)], buf.at[slot], sem.at[slot]).wait()
        o_ref[...] = buf[slot].astype(o_ref.dtype)

    @pl.when(i >= n_used)
    def _():
        o_ref[...] = jnp.zeros(o_ref.shape, o_ref.dtype)


def _dispatch_gather(tiles, src, ym, tm):
    t, d = ym.shape
    n_tiles = src.shape[0] // tm
    return pl.pallas_call(
        functools.partial(_gather_kernel, tm=tm, n_tiles=n_tiles),
        out_shape=jax.ShapeDtypeStruct((n_tiles * tm, d), BF16),
        grid_spec=pltpu.PrefetchScalarGridSpec(
            num_scalar_prefetch=2,
            grid=(n_tiles,),
            in_specs=[pl.BlockSpec(memory_space=pl.ANY)],
            out_specs=pl.BlockSpec((tm, d), lambda i, tiles, src: (i, 0)),
            scratch_shapes=[pltpu.VMEM((2, tm, d), F32), pltpu.SemaphoreType.DMA((2,))],
        ),
        compiler_params=_params(1),
        name="moe_dispatch_gather",
    )(tiles, src, ym)


def _expert_changed(tiles_ref, m):
    return jnp.logical_or(m == 0, tiles_ref[m] != tiles_ref[jnp.maximum(m - 1, 0)])


def _expert_weights(tiles_ref, n_tiles, m, copies, group_ref, on_arrival):
    @pl.when(m == 0)
    def _():
        group_ref[0] = 0
        for c in copies(tiles_ref[0], 0):
            c.start()

    @pl.when(_expert_changed(tiles_ref, m))
    def _():
        slot = group_ref[0] % 2
        for c in copies(tiles_ref[m], slot):
            c.wait()
        nxt = tiles_ref[n_tiles + 1 + m]

        @pl.when(nxt >= 0)
        def _():
            for c in copies(nxt, 1 - slot):
                c.start()

        on_arrival(slot)
        group_ref[0] = group_ref[0] + 1


def _gmm1_kernel(tiles_ref, x_ref, w_hbm, bg_ref, bu_ref, o_ref, stage, wgb_ref, wub_ref, sem, group_ref,
                 *, n_tiles, layer, nf, fn):
    n = pl.program_id(0)
    m = pl.program_id(1)

    def copies(expert, slot):
        cols = [pl.ds(pl.multiple_of(n * fn, fn), fn), pl.ds(pl.multiple_of((nf + n) * fn, fn), fn)]
        return [pltpu.make_async_copy(w_hbm.at[layer, expert, :, cols[part]], stage.at[slot, part],
                                      sem.at[slot]) for part in range(2)]

    def on_arrival(slot):
        wgb_ref[...] = stage[slot, 0].astype(BF16)
        wub_ref[...] = stage[slot, 1].astype(BF16)

    @pl.when(m < tiles_ref[n_tiles])
    def _():
        _expert_weights(tiles_ref, n_tiles, m, copies, group_ref, on_arrival)
        x = x_ref[...]
        gate = jnp.minimum(_dot(x, wgb_ref[...]) + bg_ref[...], SWIGLU_LIMIT)
        up = jnp.clip(_dot(x, wub_ref[...]) + bu_ref[...], -SWIGLU_LIMIT, SWIGLU_LIMIT)
        o_ref[...] = (gate * jax.nn.sigmoid(SWIGLU_ALPHA * gate) * (up + 1.0)).astype(o_ref.dtype)

    @pl.when(m >= tiles_ref[n_tiles])
    def _():
        o_ref[...] = jnp.zeros(o_ref.shape, o_ref.dtype)


def _gmm1(tiles, xs, w1, b1, layer, tm):
    m_rows, d = xs.shape
    n_experts, f = w1.shape[1], w1.shape[3] // 2
    fn = min(COL_TILE, f)
    nf = f // fn
    n_tiles = m_rows // tm
    b1r = b1.reshape(b1.shape[0], n_experts, 1, 2 * f)

    def expert(m, tiles):
        return tiles[_used_tile(tiles, m, n_tiles)]

    return pl.pallas_call(
        functools.partial(_gmm1_kernel, n_tiles=n_tiles, layer=layer, nf=nf, fn=fn),
        out_shape=jax.ShapeDtypeStruct((m_rows, f), BF16),
        grid_spec=pltpu.PrefetchScalarGridSpec(
            num_scalar_prefetch=1,
            grid=(nf, n_tiles),
            in_specs=[pl.BlockSpec((tm, d), lambda n, m, tl: (_used_tile(tl, m, n_tiles), 0)),
                      pl.BlockSpec(memory_space=pl.ANY),
                      pl.BlockSpec((None, None, 1, fn), lambda n, m, tl: (layer, expert(m, tl), 0, n)),
                      pl.BlockSpec((None, None, 1, fn), lambda n, m, tl: (layer, expert(m, tl), 0, nf + n))],
            out_specs=pl.BlockSpec((tm, fn), lambda n, m, tl: (m, n)),
            scratch_shapes=[pltpu.VMEM((2, 2, d, fn), F32), pltpu.VMEM((d, fn), BF16),
                            pltpu.VMEM((d, fn), BF16), pltpu.SemaphoreType.DMA((2,)),
                            pltpu.SMEM((1,), jnp.int32)],
        ),
        compiler_params=_params(2),
        name="moe_expert_up",
    )(tiles, xs, w1, b1r, b1r)


def _gmm2_kernel(tiles_ref, a_ref, w_hbm, b_ref, o_ref, stage, wb_ref, sem, group_ref,
                 *, n_tiles, layer, dn):
    n = pl.program_id(0)
    m = pl.program_id(1)

    def copies(expert, slot):
        cols = pl.ds(pl.multiple_of(n * dn, dn), dn)
        return [pltpu.make_async_copy(w_hbm.at[layer, expert, :, cols], stage.at[slot], sem.at[slot])]

    def on_arrival(slot):
        wb_ref[...] = stage[slot].astype(BF16)

    @pl.when(m < tiles_ref[n_tiles])
    def _():
        _expert_weights(tiles_ref, n_tiles, m, copies, group_ref, on_arrival)
        o_ref[...] = _dot(a_ref[...], wb_ref[...]) + b_ref[...]

    @pl.when(m >= tiles_ref[n_tiles])
    def _():
        o_ref[...] = jnp.zeros(o_ref.shape, o_ref.dtype)


def _gmm2(tiles, a, w2, b2, layer, tm):
    m_rows, f = a.shape
    n_experts, d = w2.shape[1], w2.shape[3]
    dn = min(DOWN_COL_TILE, d)
    n_tiles = m_rows // tm
    b2r = b2.reshape(b2.shape[0], n_experts, 1, d)

    def expert(m, tiles):
        return tiles[_used_tile(tiles, m, n_tiles)]

    return pl.pallas_call(
        functools.partial(_gmm2_kernel, n_tiles=n_tiles, layer=layer, dn=dn),
        out_shape=jax.ShapeDtypeStruct((m_rows, d), F32),
        grid_spec=pltpu.PrefetchScalarGridSpec(
            num_scalar_prefetch=1,
            grid=(d // dn, n_tiles),
            in_specs=[pl.BlockSpec((tm, f), lambda n, m, tl: (_used_tile(tl, m, n_tiles), 0)),
                      pl.BlockSpec(memory_space=pl.ANY),
                      pl.BlockSpec((None, None, 1, dn), lambda n, m, tl: (layer, expert(m, tl), 0, n))],
            out_specs=pl.BlockSpec((tm, dn), lambda n, m, tl: (m, n)),
            scratch_shapes=[pltpu.VMEM((2, f, dn), F32), pltpu.VMEM((f, dn), BF16),
                            pltpu.SemaphoreType.DMA((2,)), pltpu.SMEM((1,), jnp.int32)],
        ),
        compiler_params=_params(2),
        name="moe_expert_down",
    )(tiles, a, w2, b2r)


def _combine_kernel(pos_ref, y_hbm, x_ref, rt_ref, gt_ref, g_ref, sc_ref, sh_ref, *rest,
                    tmc, n_tiles, final):
    if final:
        out_ref, buf, sem = rest
    else:
        xo_ref, out_ref, buf, sem = rest
    i = pl.program_id(0)

    def issue(tile, slot):
        base = tile * (tmc * TOP_K)
        for t in range(tmc):
            for k in range(TOP_K):
                p = pos_ref[base + t * TOP_K + k]
                pltpu.make_async_copy(y_hbm.at[pl.ds(p, 1)], buf.at[slot, pl.ds(k * tmc + t, 1)],
                                      sem.at[slot]).start()

    @pl.when(i == 0)
    def _():
        issue(0, 0)

    @pl.when(i + 1 < n_tiles)
    def _():
        issue(i + 1, (i + 1) % 2)

    slot = i % 2
    pltpu.make_async_copy(y_hbm.at[pl.ds(0, TOP_K * tmc)], buf.at[slot], sem.at[slot]).wait()
    f = rt_ref[:, 0:1] * buf[slot, pl.ds(0, tmc), :]
    for k in range(1, TOP_K):
        f = f + rt_ref[:, k:k + 1] * buf[slot, pl.ds(k * tmc, tmc), :]
    xn = x_ref[...] + gt_ref[...] * f
    if final:
        y = xn * lax.rsqrt(jnp.mean(xn * xn, axis=-1, keepdims=True) + EPS)
        out_ref[...] = y * g_ref[...]
    else:
        xo_ref[...] = xn
        out_ref[...] = _rms_mod(xn, g_ref[...], sc_ref[...], sh_ref[...]).astype(out_ref.dtype)


def _combine(tok, pos, y, x, route, gt, g, sc, sh, final):
    t, d = x.shape
    tmc = COMBINE_TILE
    n_tiles = t // tmc
    tile = pl.BlockSpec((tmc, d), lambda i, pos: (i, 0))
    if final:
        out_shape = jax.ShapeDtypeStruct((t, d), F32)
        out_specs = tile
    else:
        out_shape = [jax.ShapeDtypeStruct((t, d), F32), jax.ShapeDtypeStruct((t, d), BF16)]
        out_specs = [tile, tile]
    return pl.pallas_call(
        functools.partial(_combine_kernel, tmc=tmc, n_tiles=n_tiles, final=final),
        out_shape=out_shape,
        grid_spec=pltpu.PrefetchScalarGridSpec(
            num_scalar_prefetch=1,
            grid=(n_tiles,),
            in_specs=[pl.BlockSpec(memory_space=pl.ANY), tile,
                      pl.BlockSpec((tmc, V7X_LANES), lambda i, pos: (i, 0)),
                      tok.row_spec(tmc, d), _vec_spec(d), tok.row_spec(tmc, d), tok.row_spec(tmc, d)],
            out_specs=out_specs,
            scratch_shapes=[pltpu.VMEM((2, TOP_K * tmc, d), F32), pltpu.SemaphoreType.DMA((2,))],
        ),
        compiler_params=_params(1),
        name="moe_combine",
    )(pos, y, x, route, gt, g, sc, sh)


def kernel(x_prompt, x_sample, cache_k, cache_v, c, c_ctx, w_ada, b_ada, g_mix, g_ffn, g_final,
           w_in_ab, conv_w, conv_b, rpb, w_out_ab, w_in_c, g_sgu, w_s, b_s, w_out_c,
           w_router, b_router, w1, b1, w2, b2):
    n_b, seq, d = x_prompt.shape
    n_db, dseq, _ = x_sample.shape
    depth = w_ada.shape[0]
    n_experts = w_router.shape[-1]
    heads, head_dim = cache_k.shape[3], cache_k.shape[4]
    na_width = heads * head_dim
    assert head_dim == NA_HEAD_DIM and w_in_ab.shape[-1] == 6 * na_width and d == 2 * na_width
    tok = _Tokens(n_b * seq, seq, n_db * dseq, dseq)

    x = jnp.concatenate([x_prompt.reshape(tok.n_ctx, d), x_sample.reshape(tok.n_lat, d)], axis=0)

    n_rows = 1 + n_db
    pad_rows = -n_rows % (2 * V7X_SUBLANES)
    cvec = jnp.concatenate([c_ctx[None, :], c, jnp.zeros((pad_rows, d), F32)], axis=0)
    mod = _modulation(cvec, w_ada, b_ada)[:, :n_rows]
    mod = mod.reshape(depth, n_rows, 6, 1, d).transpose(0, 2, 1, 3, 4)

    wr_pad = jnp.pad(w_router, ((0, 0), (0, 0), (0, V7X_LANES - n_experts)))
    br_pad = jnp.pad(b_router, ((0, 0), (0, V7X_LANES - n_experts)), constant_values=MASKED)
    rows = dseq // GRID_W

    xm = _first_norm(tok, x, g_mix[0][None], mod[0, 1], mod[0, 0])
    new_k, new_v = [], []
    out = None
    for l in range(depth):
        sh1, sc1, gt1, sh2, sc2, gt2 = (mod[l, s] for s in range(6))
        tail = (x, gt1, g_ffn[l][None], sc2, sh2, wr_pad[l], br_pad[l][None])
        if l % 2 == 0:
            i = l // 2
            h6 = _projection(xm, w_in_ab, i, na_width)
            att_c = _context_attention(h6, n_b, seq, na_width)
            att_l = _neighbourhood_attention(h6, cache_k, cache_v, i, _nbr_bias_blocks(rpb[i], rows),
                                             tok, n_db, na_width)
            att = jnp.concatenate([att_c, att_l], axis=0)
            x, ym, route, counts = _outproj_ab(tok, att, h6, conv_w, conv_b, w_out_ab, i, *tail)
            new_k.append(h6[1, :tok.n_ctx].reshape(n_b, seq, heads, head_dim))
            new_v.append(h6[2, :tok.n_ctx].reshape(n_b, seq, heads, head_dim))
        else:
            j = l // 2
            uv4 = _projection(xm, w_in_c, j, d // 2)
            x, ym, route, counts = _outproj_sgu(tok, uv4, g_sgu, w_s, b_s, w_out_c, j, *tail)
        pos, tiles, src = _route_metadata(route, counts, n_experts, TOKEN_TILE)
        xs = _dispatch_gather(tiles, src, ym, TOKEN_TILE)
        a = _gmm1(tiles, xs, w1, b1, l, TOKEN_TILE)
        y = _gmm2(tiles, a, w2, b2, l, TOKEN_TILE)
        if l + 1 < depth:
            x, xm = _combine(tok, pos, y, x, route, gt2, g_mix[l + 1][None],
                             mod[l + 1, 1], mod[l + 1, 0], final=False)
        else:
            out = _combine(tok, pos, y, x, route, gt2, g_final[None], gt2, gt2, final=True)

    y_prompt = out[:tok.n_ctx].reshape(n_b, seq, d)
    y_sample = out[tok.n_ctx:].reshape(n_db, dseq, d)
    return (y_prompt, y_sample, jnp.stack(new_k, axis=1), jnp.stack(new_v, axis=1))
```
